```python
import math
import jax, jax.numpy as jnp
from jax import lax
import numpy as np

D_MODEL = 2048
BATCH = 8
SEQ = 2048
DEPTH = 1

N_HEADS = 8
HEAD_DIM = 128
V_HEAD_DIM = 2 * HEAD_DIM
QK_WIDTH = N_HEADS * 2 * HEAD_DIM
ATTN_WIDTH = N_HEADS * V_HEAD_DIM
ROPE_THETA = 10000.0
Q_BLOCK = 128
SUBLN_EPS = 1e-5
LAMBDA_STD = 0.1
CONV_CH = D_MODEL
CONV_WIDTH = 31
LN_EPS = 1e-5
D_FF = 4 * D_MODEL
NORM_EPS = 1e-6
NEG_INF = -1e30
IN_WIDTH = 2 * QK_WIDTH + ATTN_WIDTH + 2 * CONV_CH + 2 * D_MODEL
SPLITS = (QK_WIDTH, 2 * QK_WIDTH, 2 * QK_WIDTH + ATTN_WIDTH,
          2 * QK_WIDTH + ATTN_WIDTH + 2 * CONV_CH,
          2 * QK_WIDTH + ATTN_WIDTH + 2 * CONV_CH + D_MODEL)

kernel_name = 'hybrid_diffattn_conformer_gated'


def rms_norm(x, gain, eps):
    x32 = x.astype(jnp.float32)
    y = x32 * lax.rsqrt(jnp.mean(x32 * x32, axis=-1, keepdims=True) + eps)
    return (y * gain.astype(jnp.float32)).astype(x.dtype)


def layer_norm(x, gain, bias, eps):
    x32 = x.astype(jnp.float32)
    mu = jnp.mean(x32, axis=-1, keepdims=True)
    xc = x32 - mu
    y = xc * lax.rsqrt(jnp.mean(xc * xc, axis=-1, keepdims=True) + eps)
    return (y * gain.astype(jnp.float32) + bias.astype(jnp.float32)).astype(x.dtype)


def rope(x, pos):
    d = x.shape[-1]
    inv_freq = 1.0 / (ROPE_THETA ** (jnp.arange(0, d, 2, dtype=jnp.float32) / d))
    ang = pos[:, None] * inv_freq[None, :]
    cos = jnp.cos(ang)[None, :, None, None, :].astype(x.dtype)
    sin = jnp.sin(ang)[None, :, None, None, :].astype(x.dtype)
    x1, x2 = jnp.split(x, 2, axis=-1)
    return jnp.concatenate([x1 * cos - x2 * sin, x2 * cos + x1 * sin], axis=-1)


def diff_attention(q, k, v, lam):
    S = q.shape[1]
    outs = []
    for i in range(S // Q_BLOCK):
        end = (i + 1) * Q_BLOCK
        qb = q[:, i * Q_BLOCK:end]
        kb = k[:, :end]
        vb = v[:, :end]
        s = jnp.einsum('bqhmd,bkhmd->bhmqk', qb, kb).astype(jnp.float32)
        q_pos = i * Q_BLOCK + jnp.arange(Q_BLOCK)
        mask = jnp.arange(end)[None, :] <= q_pos[:, None]
        p = jax.nn.softmax(jnp.where(mask, s, NEG_INF), axis=-1)
        a = p[:, :, 0] - lam * p[:, :, 1]
        outs.append(jnp.einsum('bhqk,bkhe->bqhe', a.astype(vb.dtype), vb))
    return jnp.concatenate(outs, axis=1)


def causal_depthwise_conv(u, w, b):
    C = u.shape[-1]
    y = lax.conv_general_dilated(
        u, w[:, None, :].astype(u.dtype), window_strides=(1,),
        padding=((CONV_WIDTH - 1, 0),), dimension_numbers=('NWC', 'WIO', 'NWC'),
        feature_group_count=C)
    return y + b.astype(u.dtype)


def setup_inputs(seed: int = 0) -> dict:
    key = jax.random.key(seed)
    ks = jax.random.split(key, 24)
    f32 = jnp.float32
    nrm = lambda k, shape, s: jax.random.normal(k, shape, f32) * s
    gain = lambda k, n: 1.0 + nrm(k, (DEPTH, n), 0.02)
    return {
        'x': nrm(ks[0], (BATCH, SEQ, D_MODEL), 1.0),
        'pre_mix_gain': gain(ks[1], D_MODEL),
        'w_in': nrm(ks[2], (DEPTH, D_MODEL, IN_WIDTH), D_MODEL ** -0.5),
        'lambda_q1': nrm(ks[3], (DEPTH, HEAD_DIM), LAMBDA_STD),
        'lambda_k1': nrm(ks[4], (DEPTH, HEAD_DIM), LAMBDA_STD),
        'lambda_q2': nrm(ks[5], (DEPTH, HEAD_DIM), LAMBDA_STD),
        'lambda_k2': nrm(ks[6], (DEPTH, HEAD_DIM), LAMBDA_STD),
        'subln_gain': gain(ks[7], V_HEAD_DIM),
        'glu_bias': nrm(ks[8], (DEPTH, 2 * CONV_CH), 0.02),
        'dw_kernel': nrm(ks[9], (DEPTH, CONV_WIDTH, CONV_CH), CONV_WIDTH ** -0.5),
        'dw_bias': nrm(ks[10], (DEPTH, CONV_CH), 0.02),
        'conv_ln_gain': gain(ks[11], CONV_CH),
        'conv_ln_bias': nrm(ks[12], (DEPTH, CONV_CH), 0.02),
        'w_conv_out': nrm(ks[13], (DEPTH, CONV_CH, D_MODEL), CONV_CH ** -0.5),
        'b_conv_out': nrm(ks[14], (DEPTH, D_MODEL), 0.02),
        'w_out': nrm(ks[15], (DEPTH, D_MODEL, D_MODEL), D_MODEL ** -0.5),
        'post_mix_gain': gain(ks[16], D_MODEL),
        'pre_ff_gain': gain(ks[17], D_MODEL),
        'w_ff1': nrm(ks[18], (DEPTH, D_MODEL, D_FF), D_MODEL ** -0.5),
        'w_ff2': nrm(ks[19], (DEPTH, D_FF, D_MODEL), D_FF ** -0.5),
        'post_ff_gain': gain(ks[20], D_MODEL),
    }


def reference(x, pre_mix_gain, w_in, lambda_q1, lambda_k1, lambda_q2, lambda_k2, subln_gain,
              glu_bias, dw_kernel, dw_bias, conv_ln_gain, conv_ln_bias, w_conv_out, b_conv_out,
              w_out, post_mix_gain, pre_ff_gain, w_ff1, w_ff2, post_ff_gain):
    B, S, _ = x.shape
    pos = jnp.arange(S, dtype=jnp.float32)
    h = x
    for l in range(DEPTH):
        lam_init = 0.8 - 0.6 * math.exp(-0.3 * l)
        u = rms_norm(h, pre_mix_gain[l], NORM_EPS)
        z = u @ w_in[l]
        zq, zk, zv, zc, zga, zgc = jnp.split(z, SPLITS, axis=-1)
        q = rope(zq.reshape(B, S, N_HEADS, 2, HEAD_DIM), pos) * (HEAD_DIM ** -0.5)
        k = rope(zk.reshape(B, S, N_HEADS, 2, HEAD_DIM), pos)
        v = zv.reshape(B, S, N_HEADS, V_HEAD_DIM)
        f32 = jnp.float32
        lam = (jnp.exp(jnp.sum(lambda_q1[l].astype(f32) * lambda_k1[l].astype(f32)))
               - jnp.exp(jnp.sum(lambda_q2[l].astype(f32) * lambda_k2[l].astype(f32)))
               + lam_init)
        o = diff_attention(q, k, v, lam)
        o = rms_norm(o, subln_gain[l], SUBLN_EPS) * (1.0 - lam_init)
        attn_out = o.reshape(B, S, ATTN_WIDTH)
        zc = zc + glu_bias[l]
        c_val, c_gate = jnp.split(zc, 2, axis=-1)
        c = c_val * jax.nn.sigmoid(c_gate)
        c = causal_depthwise_conv(c, dw_kernel[l], dw_bias[l])
        c = jax.nn.silu(layer_norm(c, conv_ln_gain[l], conv_ln_bias[l], LN_EPS))
        conv_out = c @ w_conv_out[l] + b_conv_out[l]
        m = jax.nn.sigmoid(zga) * attn_out + jax.nn.sigmoid(zgc) * conv_out
        h = h + rms_norm(m @ w_out[l], post_mix_gain[l], NORM_EPS)
        u = rms_norm(h, pre_ff_gain[l], NORM_EPS)
        f = jnp.square(jax.nn.relu(u @ w_ff1[l])) @ w_ff2[l]
        h = h + rms_norm(f, post_ff_gain[l], NORM_EPS)
    return h
```

```python
import functools
import math

import jax
import jax.numpy as jnp
from jax import lax
from jax.experimental import pallas as pl
from jax.experimental.pallas import tpu as pltpu

N_HEADS = 8
HEAD_DIM = 128
V_HEAD_DIM = 2 * HEAD_DIM
ROPE_THETA = 10000.0
SUBLN_EPS = 1e-5
CONV_WIDTH = 31
LN_EPS = 1e-5
NORM_EPS = 1e-6
NEG_INF = -1e30

LANES = 128
SUBLANES = 8
VMEM_LIMIT_BYTES = 56 * 1024 * 1024
HALO = 32

_BF16 = jnp.bfloat16
_F32 = jnp.float32


def _params(*semantics):
    return pltpu.CompilerParams(dimension_semantics=semantics, vmem_limit_bytes=VMEM_LIMIT_BYTES)


def _rms(v, gain, eps):
    return v * lax.rsqrt(jnp.mean(v * v, axis=-1, keepdims=True) + eps) * gain


def _inproj_kernel(x_ref, g_ref, w_ref, cos_ref, sin_ref, b_ref, z_ref, u_ref, *,
                   tn, qk_tiles, v_tiles, conv_tiles):
    j = pl.program_id(1)

    @pl.when(j == 0)
    def _():
        u_ref[...] = _rms(x_ref[...], g_ref[...], NORM_EPS).astype(_BF16)

    acc = jnp.dot(u_ref[...], w_ref[...], preferred_element_type=_F32)

    def rope_store(scale):
        cos = cos_ref[...]
        sin = sin_ref[...]
        for c in range(tn // HEAD_DIM):
            sl = slice(c * HEAD_DIM, (c + 1) * HEAD_DIM)
            a = acc[:, sl]
            r = a * cos + pltpu.roll(a, HEAD_DIM // 2, axis=1) * sin
            if scale is not None:
                r = r * scale
            z_ref[:, sl] = r.astype(_BF16)

    q_end = qk_tiles
    k_end = 2 * qk_tiles
    v_end = k_end + v_tiles
    c_end = v_end + conv_tiles

    @pl.when(j < q_end)
    def _():
        rope_store(HEAD_DIM ** -0.5)

    @pl.when((j >= q_end) & (j < k_end))
    def _():
        rope_store(None)

    @pl.when((j >= k_end) & (j < v_end))
    def _():
        z_ref[...] = acc.astype(_BF16)

    @pl.when((j >= v_end) & (j < c_end))
    def _():
        z_ref[...] = (acc + b_ref[...]).astype(_BF16)

    @pl.when(j >= c_end)
    def _():
        z_ref[...] = jax.nn.sigmoid(acc).astype(_BF16)


def _inproj(x2, gain, w_in, cos, sin, glu_bias, *, seq, tm, tn):
    T, D = x2.shape
    W = w_in.shape[1]
    qk_w = N_HEADS * 2 * HEAD_DIM
    v_w = N_HEADS * V_HEAD_DIM
    conv_w = glu_bias.shape[1]
    assert qk_w % tn == 0 and v_w % tn == 0 and conv_w % tn == 0 and W % tn == 0
    assert seq % tm == 0 and T % tm == 0
    qk_tiles, v_tiles, conv_tiles = qk_w // tn, v_w // tn, conv_w // tn
    conv0 = 2 * qk_tiles + v_tiles
    pos_blocks = seq // tm
    kern = functools.partial(_inproj_kernel, tn=tn, qk_tiles=qk_tiles, v_tiles=v_tiles,
                             conv_tiles=conv_tiles)
    return pl.pallas_call(
        kern,
        grid=(T // tm, W // tn),
        in_specs=[
            pl.BlockSpec((tm, D), lambda i, j: (i, 0)),
            pl.BlockSpec((1, D), lambda i, j: (0, 0)),
            pl.BlockSpec((D, tn), lambda i, j: (0, j)),
            pl.BlockSpec((tm, HEAD_DIM), lambda i, j: (i % pos_blocks, 0)),
            pl.BlockSpec((tm, HEAD_DIM), lambda i, j: (i % pos_blocks, 0)),
            pl.BlockSpec((1, tn), lambda i, j: (0, jnp.clip(j - conv0, 0, conv_tiles - 1))),
        ],
        out_specs=pl.BlockSpec((tm, tn), lambda i, j: (i, j)),
        out_shape=jax.ShapeDtypeStruct((T, W), _BF16),
        scratch_shapes=[pltpu.VMEM((tm, D), _BF16)],
        compiler_params=_params("parallel", "arbitrary"),
        name="inproj",
    )(x2, gain, w_in, cos, sin, glu_bias)


def _attn_kernel(lq1_ref, lk1_ref, lq2_ref, lk2_ref, g_ref, q_ref, k_ref, v_ref, o_ref, acc_ref, *,
                 tq, lam_init):
    S = q_ref.shape[0]
    d = HEAD_DIM
    lam = (jnp.exp(jnp.sum(lq1_ref[...] * lk1_ref[...], axis=-1, keepdims=True))
           - jnp.exp(jnp.sum(lq2_ref[...] * lk2_ref[...], axis=-1, keepdims=True))
           + lam_init)
    row = lax.broadcasted_iota(jnp.int32, (tq, tq), 0)
    col = lax.broadcasted_iota(jnp.int32, (tq, tq), 1)
    causal = col <= row
    contract_last = (((1,), (1,)), ((), ()))

    def block(q1, q2, kv_start, carry, masked):
        m1, l1, m2, l2 = carry
        k = k_ref[pl.ds(kv_start, tq), :]
        v = v_ref[pl.ds(kv_start, tq), :]
        s1 = lax.dot_general(q1, k[:, :d], contract_last, preferred_element_type=_F32)
        s2 = lax.dot_general(q2, k[:, d:], contract_last, preferred_element_type=_F32)
        if masked:
            s1 = jnp.where(causal, s1, NEG_INF)
            s2 = jnp.where(causal, s2, NEG_INF)
        m1n = jnp.maximum(m1, jnp.max(s1, axis=-1, keepdims=True))
        m2n = jnp.maximum(m2, jnp.max(s2, axis=-1, keepdims=True))
        a1 = jnp.exp(m1 - m1n)
        a2 = jnp.exp(m2 - m2n)
        p1 = jnp.exp(s1 - m1n)
        p2 = jnp.exp(s2 - m2n)
        l1 = a1 * l1 + jnp.sum(p1, axis=-1, keepdims=True)
        l2 = a2 * l2 + jnp.sum(p2, axis=-1, keepdims=True)
        p = jnp.concatenate([p1.astype(_BF16), p2.astype(_BF16)], axis=0)
        pv = jnp.dot(p, v, preferred_element_type=_F32)
        alpha = jnp.concatenate([a1, a2], axis=0)
        acc_ref[...] = acc_ref[...] * alpha + pv
        return m1n, l1, m2n, l2

    for i in range(S // tq):
        q = q_ref[i * tq:(i + 1) * tq, :]
        q1 = q[:, :d]
        q2 = q[:, d:]
        acc_ref[...] = jnp.zeros_like(acc_ref)
        neg = jnp.full((tq, 1), NEG_INF, _F32)
        zero = jnp.zeros((tq, 1), _F32)
        carry = (neg, zero, neg, zero)
        if i > 0:
            carry = lax.fori_loop(
                0, i,
                lambda jj, c: block(q1, q2, pl.multiple_of(jj * tq, tq), c, False),
                carry)
        m1, l1, m2, l2 = block(q1, q2, i * tq, carry, True)
        o = acc_ref[:tq, :] * (1.0 / l1) - acc_ref[tq:, :] * (lam / l2)
        o = _rms(o, g_ref[...], SUBLN_EPS) * (1.0 - lam_init)
        o_ref[i * tq:(i + 1) * tq, :] = o.astype(_BF16)


def _attention(z, lq1, lk1, lq2, lk2, subln_gain, *, batch, seq, tq, lam_init):
    T = z.shape[0]
    hw = V_HEAD_DIM
    assert seq % tq == 0
    kern = functools.partial(_attn_kernel, tq=tq, lam_init=lam_init)
    vec = pl.BlockSpec((1, HEAD_DIM), lambda b, h: (0, 0))
    return pl.pallas_call(
        kern,
        grid=(batch, N_HEADS),
        in_specs=[
            vec, vec, vec, vec,
            pl.BlockSpec((1, hw), lambda b, h: (0, 0)),
            pl.BlockSpec((seq, hw), lambda b, h: (b, h)),
            pl.BlockSpec((seq, hw), lambda b, h: (b, N_HEADS + h)),
            pl.BlockSpec((seq, hw), lambda b, h: (b, 2 * N_HEADS + h)),
        ],
        out_specs=pl.BlockSpec((seq, hw), lambda b, h: (b, h)),
        out_shape=jax.ShapeDtypeStruct((T, N_HEADS * hw), _BF16),
        scratch_shapes=[pltpu.VMEM((2 * tq, hw), _F32)],
        compiler_params=_params("parallel", "parallel"),
        name="diff_attn",
    )(lq1, lk1, lq2, lk2, subln_gain, z, z, z)


def _conv_kernel(zv_ref, zg_ref, hv_ref, hg_ref, w_ref, b_ref, lg_ref, lb_ref, o_ref, g_ref, y_ref, *,
                 rc):
    s = pl.program_id(1)
    tc, C = o_ref.shape

    def glu(val, gate):
        return val.astype(_F32) * jax.nn.sigmoid(gate.astype(_F32))

    @pl.when(s == 0)
    def _():
        g_ref[:HALO, :] = jnp.zeros((HALO, C), _F32)

    @pl.when(s > 0)
    def _():
        g_ref[:HALO, :] = glu(hv_ref[...], hg_ref[...])

    g_ref[HALO:, :] = glu(zv_ref[...], zg_ref[...])

    base = HALO - (CONV_WIDTH - 1)

    def chan_body(ci, _):
        c0 = pl.multiple_of(ci * LANES, LANES)

        def row_body(ri, _):
            r0 = pl.multiple_of(ri * rc, rc)
            win = g_ref[pl.ds(r0, rc + HALO), pl.ds(c0, LANES)]
            acc = jnp.zeros((rc, LANES), _F32)
            for r in range(SUBLANES):
                taps = [k for k in range(CONV_WIDTH) if (base + k) % SUBLANES == r]
                shifted = win[r:r + rc + HALO - SUBLANES, :] if r else win
                for k in taps:
                    a = (base + k) // SUBLANES * SUBLANES
                    acc = acc + w_ref[k:k + 1, pl.ds(c0, LANES)] * shifted[a:a + rc, :]
            y_ref[pl.ds(r0, rc), pl.ds(c0, LANES)] = acc
            return 0

        return lax.fori_loop(0, tc // rc, row_body, 0)

    lax.fori_loop(0, C // LANES, chan_body, 0)

    y = y_ref[...] + b_ref[...]
    mu = jnp.mean(y, axis=-1, keepdims=True)
    yc = y - mu
    yn = yc * lax.rsqrt(jnp.mean(yc * yc, axis=-1, keepdims=True) + LN_EPS)
    yn = yn * lg_ref[...] + lb_ref[...]
    o_ref[...] = (yn * jax.nn.sigmoid(yn)).astype(_BF16)


def _conv_branch(z, dw_kernel, dw_bias, ln_gain, ln_bias, *, batch, seq, tc, val_col, rc=32):
    T = z.shape[0]
    C = dw_bias.shape[1]
    assert seq % tc == 0 and tc % HALO == 0 and tc % rc == 0 and val_col % C == 0
    vb = val_col // C
    sblocks = seq // tc
    hpb = tc // HALO

    def halo_map(col):
        return lambda b, s: (jnp.maximum((b * sblocks + s) * hpb - 1, 0), col)

    vec = pl.BlockSpec((1, C), lambda b, s: (0, 0))
    kern = functools.partial(_conv_kernel, rc=rc)
    return pl.pallas_call(
        kern,
        grid=(batch, sblocks),
        in_specs=[
            pl.BlockSpec((tc, C), lambda b, s: (b * sblocks + s, vb)),
            pl.BlockSpec((tc, C), lambda b, s: (b * sblocks + s, vb + 1)),
            pl.BlockSpec((HALO, C), halo_map(vb)),
            pl.BlockSpec((HALO, C), halo_map(vb + 1)),
            pl.BlockSpec((CONV_WIDTH, C), lambda b, s: (0, 0)),
            vec, vec, vec,
        ],
        out_specs=pl.BlockSpec((tc, C), lambda b, s: (b * sblocks + s, 0)),
        out_shape=jax.ShapeDtypeStruct((T, C), _BF16),
        scratch_shapes=[pltpu.VMEM((tc + HALO, C), _F32), pltpu.VMEM((tc, C), _F32)],
        compiler_params=_params("parallel", "arbitrary"),
        name="conv_branch",
    )(z, z, z, z, dw_kernel, dw_bias, ln_gain, ln_bias)


def _merge_kernel(c_ref, a_ref, ga_ref, gc_ref, x_ref, wc_ref, bc_ref, wo_ref, gpost_ref, gff_ref,
                  h_ref, u_ref):
    conv_out = jnp.dot(c_ref[...], wc_ref[...], preferred_element_type=_F32) + bc_ref[...]
    m = ga_ref[...].astype(_F32) * a_ref[...].astype(_F32) + gc_ref[...].astype(_F32) * conv_out
    y = jnp.dot(m.astype(_BF16), wo_ref[...], preferred_element_type=_F32)
    h = x_ref[...] + _rms(y, gpost_ref[...], NORM_EPS)
    h_ref[...] = h
    u_ref[...] = _rms(h, gff_ref[...], NORM_EPS).astype(_BF16)


def _merge_proj(c_act, attn, z, x2, w_conv_out, b_conv_out, w_out, g_post, g_ff, *, tm, ga_col):
    T, D = x2.shape
    assert T % tm == 0 and ga_col % D == 0
    gb = ga_col // D
    row = lambda col: pl.BlockSpec((tm, D), lambda i: (i, col))
    const = lambda shape: pl.BlockSpec(shape, lambda i: (0, 0), pipeline_mode=pl.Buffered(1))
    return pl.pallas_call(
        _merge_kernel,
        grid=(T // tm,),
        in_specs=[row(0), row(0), row(gb), row(gb + 1), row(0),
                  const((D, D)), const((1, D)), const((D, D)), const((1, D)), const((1, D))],
        out_specs=[row(0), row(0)],
        out_shape=[jax.ShapeDtypeStruct((T, D), _F32), jax.ShapeDtypeStruct((T, D), _BF16)],
        compiler_params=_params("parallel"),
        name="merge_proj",
    )(c_act, attn, z, z, x2, w_conv_out, b_conv_out, w_out, g_post, g_ff)


def _ffn_kernel(u_ref, w1_ref, w2_ref, h_ref, g_ref, o_ref, acc_ref):
    j = pl.program_id(1)
    a = jnp.dot(u_ref[...], w1_ref[...], preferred_element_type=_F32)
    a = jnp.square(jnp.maximum(a, 0.0)).astype(_BF16)
    part = jnp.dot(a, w2_ref[...], preferred_element_type=_F32)

    @pl.when(j == 0)
    def _():
        acc_ref[...] = part

    @pl.when(j > 0)
    def _():
        acc_ref[...] += part

    @pl.when(j == pl.num_programs(1) - 1)
    def _():
        o_ref[...] = h_ref[...] + _rms(acc_ref[...], g_ref[...], NORM_EPS)


def _ffn(u2, w1, w2, h1, gain, *, tm, tf):
    T, D = h1.shape
    F = w1.shape[1]
    assert T % tm == 0 and F % tf == 0
    return pl.pallas_call(
        _ffn_kernel,
        grid=(T // tm, F // tf),
        in_specs=[
            pl.BlockSpec((tm, D), lambda i, j: (i, 0)),
            pl.BlockSpec((D, tf), lambda i, j: (0, j)),
            pl.BlockSpec((tf, D), lambda i, j: (j, 0)),
            pl.BlockSpec((tm, D), lambda i, j: (i, 0)),
            pl.BlockSpec((1, D), lambda i, j: (0, 0)),
        ],
        out_specs=pl.BlockSpec((tm, D), lambda i, j: (i, 0)),
        out_shape=jax.ShapeDtypeStruct((T, D), _F32),
        scratch_shapes=[pltpu.VMEM((tm, D), _F32)],
        compiler_params=_params("parallel", "arbitrary"),
        name="ffn",
    )(u2, w1, w2, h1, gain)


def _rope_tables(seq):
    d = HEAD_DIM
    inv_freq = 1.0 / (ROPE_THETA ** (jnp.arange(0, d, 2, dtype=_F32) / d))
    ang = jnp.arange(seq, dtype=_F32)[:, None] * inv_freq[None, :]
    cos = jnp.cos(ang)
    sin = jnp.sin(ang)
    return jnp.concatenate([cos, cos], axis=-1), jnp.concatenate([-sin, sin], axis=-1)


def _layer(h2d, l, p, *, batch, seq):
    D = h2d.shape[1]
    lam_init = 0.8 - 0.6 * math.exp(-0.3 * l)
    row = lambda a: a[l][None, :].astype(_F32)
    cos, sin = _rope_tables(seq)
    qk_w = N_HEADS * 2 * HEAD_DIM
    conv_col = 2 * qk_w + N_HEADS * V_HEAD_DIM
    conv_w = p['glu_bias'].shape[1]

    tm_in = min(1024, seq)
    z = _inproj(h2d, row(p['pre_mix_gain']), p['w_in'][l].astype(_BF16), cos, sin,
                row(p['glu_bias']), seq=seq, tm=tm_in, tn=1024)
    attn = _attention(z, row(p['lambda_q1']), row(p['lambda_k1']), row(p['lambda_q2']),
                      row(p['lambda_k2']), row(p['subln_gain']),
                      batch=batch, seq=seq, tq=min(512, seq), lam_init=lam_init)
    c_act = _conv_branch(z, p['dw_kernel'][l].astype(_F32), row(p['dw_bias']),
                         row(p['conv_ln_gain']), row(p['conv_ln_bias']),
                         batch=batch, seq=seq, tc=min(256, seq), val_col=conv_col)
    h1, u2 = _merge_proj(c_act, attn, z, h2d, p['w_conv_out'][l].astype(_BF16),
                         row(p['b_conv_out']), p['w_out'][l].astype(_BF16),
                         row(p['post_mix_gain']), row(p['pre_ff_gain']),
                         tm=min(256, seq), ga_col=conv_col + conv_w)
    return _ffn(u2, p['w_ff1'][l].astype(_BF16), p['w_ff2'][l].astype(_BF16), h1,
                row(p['post_ff_gain']), tm=min(512, seq), tf=1024)


def kernel(x, pre_mix_gain, w_in, lambda_q1, lambda_k1, lambda_q2, lambda_k2, subln_gain, glu_bias,
           dw_kernel, dw_bias, conv_ln_gain, conv_ln_bias, w_conv_out, b_conv_out, w_out,
           post_mix_gain, pre_ff_gain, w_ff1, w_ff2, post_ff_gain):
    B, S, D = x.shape
    p = dict(pre_mix_gain=pre_mix_gain, w_in=w_in, lambda_q1=lambda_q1, lambda_k1=lambda_k1,
             lambda_q2=lambda_q2, lambda_k2=lambda_k2, subln_gain=subln_gain, glu_bias=glu_bias,
             dw_kernel=dw_kernel, dw_bias=dw_bias, conv_ln_gain=conv_ln_gain,
             conv_ln_bias=conv_ln_bias, w_conv_out=w_conv_out, b_conv_out=b_conv_out, w_out=w_out,
             post_mix_gain=post_mix_gain, pre_ff_gain=pre_ff_gain, w_ff1=w_ff1, w_ff2=w_ff2,
             post_ff_gain=post_ff_gain)
    h = x.reshape(B * S, D).astype(_F32)
    for l in range(w_in.shape[0]):
        h = _layer(h, l, p, batch=B, seq=S)
    return h.reshape(B, S, D).astype(x.dtype)
```

```python
import functools
import math

import jax
import jax.numpy as jnp
from jax import lax
from jax.experimental import pallas as pl
from jax.experimental.pallas import tpu as pltpu

N_HEADS = 8
HEAD_DIM = 128
V_HEAD_DIM = 2 * HEAD_DIM
ROPE_THETA = 10000.0
SUBLN_EPS = 1e-5
CONV_WIDTH = 31
LN_EPS = 1e-5
NORM_EPS = 1e-6
NEG_INF = -1e30
LOG2_E = 1.4426950408889634

LANES = 128
SUBLANES = 8
BF16_ROWS = 16
VMEM_LIMIT_BYTES = 56 * 1024 * 1024
HALO = 32

_BF16 = jnp.bfloat16
_F32 = jnp.float32


def _params(*semantics):
    return pltpu.CompilerParams(dimension_semantics=semantics, vmem_limit_bytes=VMEM_LIMIT_BYTES)


def _rms(v, gain, eps):
    return v * lax.rsqrt(jnp.mean(v * v, axis=-1, keepdims=True) + eps) * gain


def _inproj_kernel(x_ref, g_ref, w_ref, cos_ref, sin_ref, b_ref, z_ref, u_ref, *,
                   tn, qk_tiles, v_tiles, conv_tiles):
    j = pl.program_id(1)

    @pl.when(j == 0)
    def _():
        u_ref[...] = _rms(x_ref[...], g_ref[...], NORM_EPS).astype(_BF16)

    acc = jnp.dot(u_ref[...], w_ref[...], preferred_element_type=_F32)

    def rope_store(scale):
        cos = cos_ref[...]
        sin = sin_ref[...]
        for c in range(tn // HEAD_DIM):
            sl = slice(c * HEAD_DIM, (c + 1) * HEAD_DIM)
            a = acc[:, sl]
            r = a * cos + pltpu.roll(a, HEAD_DIM // 2, axis=1) * sin
            if scale is not None:
                r = r * scale
            z_ref[:, sl] = r.astype(_BF16)

    q_end = qk_tiles
    k_end = 2 * qk_tiles
    v_end = k_end + v_tiles
    c_end = v_end + conv_tiles

    @pl.when(j < q_end)
    def _():
        rope_store(HEAD_DIM ** -0.5 * LOG2_E)

    @pl.when((j >= q_end) & (j < k_end))
    def _():
        rope_store(None)

    @pl.when((j >= k_end) & (j < v_end))
    def _():
        z_ref[...] = acc.astype(_BF16)

    @pl.when((j >= v_end) & (j < c_end))
    def _():
        z_ref[...] = (acc + b_ref[...]).astype(_BF16)

    @pl.when(j >= c_end)
    def _():
        z_ref[...] = jax.nn.sigmoid(acc).astype(_BF16)


def _inproj(x2, gain, w_in, cos, sin, glu_bias, *, seq, tm, tn):
    T, D = x2.shape
    W = w_in.shape[1]
    qk_w = N_HEADS * 2 * HEAD_DIM
    v_w = N_HEADS * V_HEAD_DIM
    conv_w = glu_bias.shape[1]
    assert qk_w % tn == 0 and v_w % tn == 0 and conv_w % tn == 0 and W % tn == 0
    assert seq % tm == 0 and T % tm == 0
    qk_tiles, v_tiles, conv_tiles = qk_w // tn, v_w // tn, conv_w // tn
    conv0 = 2 * qk_tiles + v_tiles
    pos_blocks = seq // tm
    kern = functools.partial(_inproj_kernel, tn=tn, qk_tiles=qk_tiles, v_tiles=v_tiles,
                             conv_tiles=conv_tiles)
    return pl.pallas_call(
        kern,
        grid=(T // tm, W // tn),
        in_specs=[
            pl.BlockSpec((tm, D), lambda i, j: (i, 0)),
            pl.BlockSpec((1, D), lambda i, j: (0, 0)),
            pl.BlockSpec((D, tn), lambda i, j: (0, j)),
            pl.BlockSpec((tm, HEAD_DIM), lambda i, j: (i % pos_blocks, 0)),
            pl.BlockSpec((tm, HEAD_DIM), lambda i, j: (i % pos_blocks, 0)),
            pl.BlockSpec((1, tn), lambda i, j: (0, jnp.clip(j - conv0, 0, conv_tiles - 1))),
        ],
        out_specs=pl.BlockSpec((tm, tn), lambda i, j: (i, j)),
        out_shape=jax.ShapeDtypeStruct((T, W), _BF16),
        scratch_shapes=[pltpu.VMEM((tm, D), _BF16)],
        compiler_params=_params("parallel", "arbitrary"),
        name="inproj",
    )(x2, gain, w_in, cos, sin, glu_bias)


def _attn_kernel(lq1_ref, lk1_ref, lq2_ref, lk2_ref, g_ref, q_ref, k_ref, v_ref, o_ref,
                 s_ref, a_ref, part_ref, stat_ref, *, tq, lam_init):
    S = q_ref.shape[0]
    d = HEAD_DIM
    rc = BF16_ROWS
    lam = (jnp.exp(jnp.sum(lq1_ref[...] * lk1_ref[...], axis=-1, keepdims=True))
           - jnp.exp(jnp.sum(lq2_ref[...] * lk2_ref[...], axis=-1, keepdims=True))
           + lam_init)
    row = lax.broadcasted_iota(jnp.int32, (tq, tq), 0)
    col = lax.broadcasted_iota(jnp.int32, (tq, tq), 1)
    causal = col <= row
    contract_last = (((1,), (1,)), ((), ()))

    for i in range(S // tq):
        off = i * tq
        end = off + tq
        ncol = end // LANES
        q = q_ref[off:end, :]
        for h in range(2):
            qh = q[:, h * d:(h + 1) * d]
            if off:
                s_ref[h, :, :off] = lax.dot_general(qh, k_ref[:off, h * d:(h + 1) * d], contract_last,
                                                    preferred_element_type=_F32)
            sd = lax.dot_general(qh, k_ref[off:end, h * d:(h + 1) * d], contract_last,
                                 preferred_element_type=_F32)
            s_ref[h, :, off:end] = jnp.where(causal, sd, NEG_INF)

        def lane_block(ref, h, r0, c):
            return ref[h, pl.ds(r0, rc), c * LANES:(c + 1) * LANES]

        def max_pass(r, _):
            r0 = pl.multiple_of(r * rc, rc)
            for h in range(2):
                m = lane_block(s_ref, h, r0, 0)
                for c in range(1, ncol):
                    m = jnp.maximum(m, lane_block(s_ref, h, r0, c))
                part_ref[h, pl.ds(r0, rc), :] = m
            return 0

        lax.fori_loop(0, tq // rc, max_pass, 0)
        for h in range(2):
            m = jnp.max(part_ref[h], axis=-1, keepdims=True)
            stat_ref[h] = jnp.broadcast_to(m, (tq, LANES))

        def exp_pass(r, _):
            r0 = pl.multiple_of(r * rc, rc)
            for h in range(2):
                m = stat_ref[h, pl.ds(r0, rc), :]
                acc = jnp.zeros((rc, LANES), _F32)
                for c in range(ncol):
                    p = jnp.exp2(lane_block(s_ref, h, r0, c) - m)
                    s_ref[h, pl.ds(r0, rc), c * LANES:(c + 1) * LANES] = p
                    acc = acc + p
                part_ref[h, pl.ds(r0, rc), :] = acc
            return 0

        lax.fori_loop(0, tq // rc, exp_pass, 0)
        l1 = jnp.sum(part_ref[0], axis=-1, keepdims=True)
        l2 = jnp.sum(part_ref[1], axis=-1, keepdims=True)
        stat_ref[0] = jnp.broadcast_to(1.0 / l1, (tq, LANES))
        stat_ref[1] = jnp.broadcast_to(lam / l2, (tq, LANES))

        def combine_pass(r, _):
            r0 = pl.multiple_of(r * rc, rc)
            c1 = stat_ref[0, pl.ds(r0, rc), :]
            c2 = stat_ref[1, pl.ds(r0, rc), :]
            for c in range(ncol):
                a = lane_block(s_ref, 0, r0, c) * c1 - lane_block(s_ref, 1, r0, c) * c2
                a_ref[pl.ds(r0, rc), c * LANES:(c + 1) * LANES] = a.astype(_BF16)
            return 0

        lax.fori_loop(0, tq // rc, combine_pass, 0)
        o = jnp.dot(a_ref[:, :end], v_ref[:end, :], preferred_element_type=_F32)
        o = _rms(o, g_ref[...], SUBLN_EPS) * (1.0 - lam_init)
        o_ref[off:end, :] = o.astype(_BF16)


def _attention(z, lq1, lk1, lq2, lk2, subln_gain, *, batch, seq, tq, lam_init):
    T = z.shape[0]
    hw = V_HEAD_DIM
    assert seq % tq == 0 and tq % LANES == 0
    kern = functools.partial(_attn_kernel, tq=tq, lam_init=lam_init)
    vec = pl.BlockSpec((1, HEAD_DIM), lambda b, h: (0, 0))
    return pl.pallas_call(
        kern,
        grid=(batch, N_HEADS),
        in_specs=[
            vec, vec, vec, vec,
            pl.BlockSpec((1, hw), lambda b, h: (0, 0)),
            pl.BlockSpec((seq, hw), lambda b, h: (b, h)),
            pl.BlockSpec((seq, hw), lambda b, h: (b, N_HEADS + h)),
            pl.BlockSpec((seq, hw), lambda b, h: (b, 2 * N_HEADS + h)),
        ],
        out_specs=pl.BlockSpec((seq, hw), lambda b, h: (b, h)),
        out_shape=jax.ShapeDtypeStruct((T, N_HEADS * hw), _BF16),
        scratch_shapes=[pltpu.VMEM((2, tq, seq), _F32), pltpu.VMEM((tq, seq), _BF16),
                        pltpu.VMEM((2, tq, LANES), _F32), pltpu.VMEM((2, tq, LANES), _F32)],
        compiler_params=_params("parallel", "parallel"),
        name="diff_attn",
    )(lq1, lk1, lq2, lk2, subln_gain, z, z, z)


def _conv_kernel(zv_ref, zg_ref, hv_ref, hg_ref, w_ref, b_ref, lg_ref, lb_ref, o_ref, g_ref, y_ref, *,
                 rc):
    s = pl.program_id(1)
    tc, C = o_ref.shape

    def glu(val, gate):
        return val.astype(_F32) * jax.nn.sigmoid(gate.astype(_F32))

    @pl.when(s == 0)
    def _():
        g_ref[:HALO, :] = jnp.zeros((HALO, C), _F32)

    @pl.when(s > 0)
    def _():
        g_ref[:HALO, :] = glu(hv_ref[...], hg_ref[...])

    g_ref[HALO:, :] = glu(zv_ref[...], zg_ref[...])

    base = HALO - (CONV_WIDTH - 1)

    def chan_body(ci, _):
        c0 = pl.multiple_of(ci * LANES, LANES)

        def row_body(ri, _):
            r0 = pl.multiple_of(ri * rc, rc)
            win = g_ref[pl.ds(r0, rc + HALO), pl.ds(c0, LANES)]
            acc = None
            for r in range(SUBLANES):
                part = None
                for k in range(CONV_WIDTH):
                    if (base + k) % SUBLANES == r:
                        term = w_ref[k:k + 1, pl.ds(c0, LANES)] * win[base + k:base + k + rc, :]
                        part = term if part is None else part + term
                acc = part if acc is None else acc + part
            y_ref[pl.ds(r0, rc), pl.ds(c0, LANES)] = acc
            return 0

        return lax.fori_loop(0, tc // rc, row_body, 0)

    lax.fori_loop(0, C // LANES, chan_body, 0)

    y = y_ref[...] + b_ref[...]
    mu = jnp.mean(y, axis=-1, keepdims=True)
    yc = y - mu
    yn = yc * lax.rsqrt(jnp.mean(yc * yc, axis=-1, keepdims=True) + LN_EPS)
    yn = yn * lg_ref[...] + lb_ref[...]
    o_ref[...] = (yn * jax.nn.sigmoid(yn)).astype(_BF16)


def _conv_branch(z, dw_kernel, dw_bias, ln_gain, ln_bias, *, batch, seq, tc, val_col, rc=64):
    T = z.shape[0]
    C = dw_bias.shape[1]
    assert seq % tc == 0 and tc % HALO == 0 and tc % rc == 0 and val_col % C == 0
    vb = val_col // C
    sblocks = seq // tc
    hpb = tc // HALO

    def halo_map(col):
        return lambda b, s: (jnp.maximum((b * sblocks + s) * hpb - 1, 0), col)

    vec = pl.BlockSpec((1, C), lambda b, s: (0, 0))
    kern = functools.partial(_conv_kernel, rc=rc)
    return pl.pallas_call(
        kern,
        grid=(batch, sblocks),
        in_specs=[
            pl.BlockSpec((tc, C), lambda b, s: (b * sblocks + s, vb)),
            pl.BlockSpec((tc, C), lambda b, s: (b * sblocks + s, vb + 1)),
            pl.BlockSpec((HALO, C), halo_map(vb)),
            pl.BlockSpec((HALO, C), halo_map(vb + 1)),
            pl.BlockSpec((CONV_WIDTH, C), lambda b, s: (0, 0)),
            vec, vec, vec,
        ],
        out_specs=pl.BlockSpec((tc, C), lambda b, s: (b * sblocks + s, 0)),
        out_shape=jax.ShapeDtypeStruct((T, C), _BF16),
        scratch_shapes=[pltpu.VMEM((tc + HALO, C), _F32), pltpu.VMEM((tc, C), _F32)],
        compiler_params=_params("parallel", "arbitrary"),
        name="conv_branch",
    )(z, z, z, z, dw_kernel, dw_bias, ln_gain, ln_bias)


def _merge_kernel(c_ref, a_ref, ga_ref, gc_ref, x_ref, wc_ref, bc_ref, wo_ref, gpost_ref, gff_ref,
                  h_ref, u_ref):
    conv_out = jnp.dot(c_ref[...], wc_ref[...], preferred_element_type=_F32) + bc_ref[...]
    m = ga_ref[...].astype(_F32) * a_ref[...].astype(_F32) + gc_ref[...].astype(_F32) * conv_out
    y = jnp.dot(m.astype(_BF16), wo_ref[...], preferred_element_type=_F32)
    h = x_ref[...] + _rms(y, gpost_ref[...], NORM_EPS)
    h_ref[...] = h
    u_ref[...] = _rms(h, gff_ref[...], NORM_EPS).astype(_BF16)


def _merge_proj(c_act, attn, z, x2, w_conv_out, b_conv_out, w_out, g_post, g_ff, *, tm, ga_col):
    T, D = x2.shape
    assert T % tm == 0 and ga_col % D == 0
    gb = ga_col // D
    row = lambda col: pl.BlockSpec((tm, D), lambda i: (i, col))
    const = lambda shape: pl.BlockSpec(shape, lambda i: (0, 0), pipeline_mode=pl.Buffered(1))
    return pl.pallas_call(
        _merge_kernel,
        grid=(T // tm,),
        in_specs=[row(0), row(0), row(gb), row(gb + 1), row(0),
                  const((D, D)), const((1, D)), const((D, D)), const((1, D)), const((1, D))],
        out_specs=[row(0), row(0)],
        out_shape=[jax.ShapeDtypeStruct((T, D), _F32), jax.ShapeDtypeStruct((T, D), _BF16)],
        compiler_params=_params("parallel"),
        name="merge_proj",
    )(c_act, attn, z, z, x2, w_conv_out, b_conv_out, w_out, g_post, g_ff)


def _ffn_kernel(u_ref, w1_ref, w2_ref, h_ref, g_ref, o_ref, acc_ref):
    j = pl.program_id(1)
    a = jnp.dot(u_ref[...], w1_ref[...], preferred_element_type=_F32)
    a = jnp.square(jnp.maximum(a, 0.0)).astype(_BF16)
    part = jnp.dot(a, w2_ref[...], preferred_element_type=_F32)

    @pl.when(j == 0)
    def _():
        acc_ref[...] = part

    @pl.when(j > 0)
    def _():
        acc_ref[...] += part

    @pl.when(j == pl.num_programs(1) - 1)
    def _():
        o_ref[...] = h_ref[...] + _rms(acc_ref[...], g_ref[...], NORM_EPS)


def _ffn(u2, w1, w2, h1, gain, *, tm, tf):
    T, D = h1.shape
    F = w1.shape[1]
    assert T % tm == 0 and F % tf == 0
    return pl.pallas_call(
        _ffn_kernel,
        grid=(T // tm, F // tf),
        in_specs=[
            pl.BlockSpec((tm, D), lambda i, j: (i, 0)),
            pl.BlockSpec((D, tf), lambda i, j: (0, j)),
            pl.BlockSpec((tf, D), lambda i, j: (j, 0)),
            pl.BlockSpec((tm, D), lambda i, j: (i, 0)),
            pl.BlockSpec((1, D), lambda i, j: (0, 0)),
        ],
        out_specs=pl.BlockSpec((tm, D), lambda i, j: (i, 0)),
        out_shape=jax.ShapeDtypeStruct((T, D), _F32),
        scratch_shapes=[pltpu.VMEM((tm, D), _F32)],
        compiler_params=_params("parallel", "arbitrary"),
        name="ffn",
    )(u2, w1, w2, h1, gain)


def _rope_tables(seq):
    d = HEAD_DIM
    inv_freq = 1.0 / (ROPE_THETA ** (jnp.arange(0, d, 2, dtype=_F32) / d))
    ang = jnp.arange(seq, dtype=_F32)[:, None] * inv_freq[None, :]
    cos = jnp.cos(ang)
    sin = jnp.sin(ang)
    return jnp.concatenate([cos, cos], axis=-1), jnp.concatenate([-sin, sin], axis=-1)


def _layer(h2d, l, p, *, batch, seq):
    D = h2d.shape[1]
    lam_init = 0.8 - 0.6 * math.exp(-0.3 * l)
    row = lambda a: a[l][None, :].astype(_F32)
    cos, sin = _rope_tables(seq)
    qk_w = N_HEADS * 2 * HEAD_DIM
    conv_col = 2 * qk_w + N_HEADS * V_HEAD_DIM
    conv_w = p['glu_bias'].shape[1]

    tm_in = min(1024, seq)
    z = _inproj(h2d, row(p['pre_mix_gain']), p['w_in'][l].astype(_BF16), cos, sin,
                row(p['glu_bias']), seq=seq, tm=tm_in, tn=1024)
    attn = _attention(z, row(p['lambda_q1']), row(p['lambda_k1']), row(p['lambda_q2']),
                      row(p['lambda_k2']), row(p['subln_gain']),
                      batch=batch, seq=seq, tq=min(512, seq), lam_init=lam_init)
    c_act = _conv_branch(z, p['dw_kernel'][l].astype(_F32), row(p['dw_bias']),
                         row(p['conv_ln_gain']), row(p['conv_ln_bias']),
                         batch=batch, seq=seq, tc=min(256, seq), val_col=conv_col)
    h1, u2 = _merge_proj(c_act, attn, z, h2d, p['w_conv_out'][l].astype(_BF16),
                         row(p['b_conv_out']), p['w_out'][l].astype(_BF16),
                         row(p['post_mix_gain']), row(p['pre_ff_gain']),
                         tm=min(256, seq), ga_col=conv_col + conv_w)
    return _ffn(u2, p['w_ff1'][l].astype(_BF16), p['w_ff2'][l].astype(_BF16), h1,
                row(p['post_ff_gain']), tm=min(512, seq), tf=1024)


def kernel(x, pre_mix_gain, w_in, lambda_q1, lambda_k1, lambda_q2, lambda_k2, subln_gain, glu_bias,
           dw_kernel, dw_bias, conv_ln_gain, conv_ln_bias, w_conv_out, b_conv_out, w_out,
           post_mix_gain, pre_ff_gain, w_ff1, w_ff2, post_ff_gain):
    B, S, D = x.shape
    p = dict(pre_mix_gain=pre_mix_gain, w_in=w_in, lambda_q1=lambda_q1, lambda_k1=lambda_k1,
             lambda_q2=lambda_q2, lambda_k2=lambda_k2, subln_gain=subln_gain, glu_bias=glu_bias,
             dw_kernel=dw_kernel, dw_bias=dw_bias, conv_ln_gain=conv_ln_gain,
             conv_ln_bias=conv_ln_bias, w_conv_out=w_conv_out, b_conv_out=b_conv_out, w_out=w_out,
             post_mix_gain=post_mix_gain, pre_ff_gain=pre_ff_gain, w_ff1=w_ff1, w_ff2=w_ff2,
             post_ff_gain=post_ff_gain)
    h = x.reshape(B * S, D).astype(_F32)
    for l in range(w_in.shape[0]):
        h = _layer(h, l, p, batch=B, seq=S)
    return h.reshape(B, S, D).astype(x.dtype)
```

```python
import functools
import math

import jax
import jax.numpy as jnp
from jax import lax
from jax.experimental import pallas as pl
from jax.experimental.pallas import tpu as pltpu

N_HEADS = 8
HEAD_DIM = 128
V_HEAD_DIM = 2 * HEAD_DIM
ROPE_THETA = 10000.0
SUBLN_EPS = 1e-5
CONV_WIDTH = 31
LN_EPS = 1e-5
NORM_EPS = 1e-6
NEG_INF = -1e30
LOG2_E = 1.4426950408889634

LANES = 128
SUBLANES = 8
BF16_ROWS = 16
VMEM_LIMIT_BYTES = 56 * 1024 * 1024
HALO = 32

_BF16 = jnp.bfloat16
_F32 = jnp.float32


def _params(*semantics):
    return pltpu.CompilerParams(dimension_semantics=semantics, vmem_limit_bytes=VMEM_LIMIT_BYTES)


def _rms(v, gain, eps):
    return v * lax.rsqrt(jnp.mean(v * v, axis=-1, keepdims=True) + eps) * gain


def _inproj_kernel(x_ref, g_ref, w_ref, cos_ref, sin_ref, b_ref, z_ref, u_ref, *,
                   tn, qk_tiles, v_tiles, conv_tiles):
    j = pl.program_id(1)

    @pl.when(j == 0)
    def _():
        u_ref[...] = _rms(x_ref[...], g_ref[...], NORM_EPS).astype(_BF16)

    def matmul():
        return jnp.dot(u_ref[...], w_ref[...], preferred_element_type=_F32)

    def rope_store(scale):
        acc = matmul()
        cos = cos_ref[...]
        sin = sin_ref[...]
        for c in range(tn // HEAD_DIM):
            sl = slice(c * HEAD_DIM, (c + 1) * HEAD_DIM)
            a = acc[:, sl]
            r = a * cos + pltpu.roll(a, HEAD_DIM // 2, axis=1) * sin
            if scale is not None:
                r = r * scale
            z_ref[:, sl] = r.astype(_BF16)

    q_end = qk_tiles
    k_end = 2 * qk_tiles
    v_end = k_end + v_tiles
    c_end = v_end + conv_tiles

    @pl.when(j < q_end)
    def _():
        rope_store(HEAD_DIM ** -0.5 * LOG2_E)

    @pl.when((j >= q_end) & (j < k_end))
    def _():
        rope_store(None)

    @pl.when((j >= k_end) & (j < v_end))
    def _():
        z_ref[...] = matmul().astype(_BF16)

    @pl.when((j >= v_end) & (j < c_end))
    def _():
        z_ref[...] = (matmul() + b_ref[...]).astype(_BF16)

    @pl.when(j >= c_end)
    def _():
        z_ref[...] = jax.nn.sigmoid(matmul()).astype(_BF16)


def _inproj(x2, gain, w_in, cos, sin, glu_bias, *, seq, tm, tn):
    T, D = x2.shape
    W = w_in.shape[1]
    qk_w = N_HEADS * 2 * HEAD_DIM
    v_w = N_HEADS * V_HEAD_DIM
    conv_w = glu_bias.shape[1]
    assert qk_w % tn == 0 and v_w % tn == 0 and conv_w % tn == 0 and W % tn == 0
    assert seq % tm == 0 and T % tm == 0
    qk_tiles, v_tiles, conv_tiles = qk_w // tn, v_w // tn, conv_w // tn
    conv0 = 2 * qk_tiles + v_tiles
    pos_blocks = seq // tm
    kern = functools.partial(_inproj_kernel, tn=tn, qk_tiles=qk_tiles, v_tiles=v_tiles,
                             conv_tiles=conv_tiles)
    return pl.pallas_call(
        kern,
        grid=(T // tm, W // tn),
        in_specs=[
            pl.BlockSpec((tm, D), lambda i, j: (i, 0)),
            pl.BlockSpec((1, D), lambda i, j: (0, 0)),
            pl.BlockSpec((D, tn), lambda i, j: (0, j)),
            pl.BlockSpec((tm, HEAD_DIM), lambda i, j: (i % pos_blocks, 0)),
            pl.BlockSpec((tm, HEAD_DIM), lambda i, j: (i % pos_blocks, 0)),
            pl.BlockSpec((1, tn), lambda i, j: (0, jnp.clip(j - conv0, 0, conv_tiles - 1))),
        ],
        out_specs=pl.BlockSpec((tm, tn), lambda i, j: (i, j)),
        out_shape=jax.ShapeDtypeStruct((T, W), _BF16),
        scratch_shapes=[pltpu.VMEM((tm, D), _BF16)],
        compiler_params=_params("parallel", "arbitrary"),
        name="inproj",
    )(x2, gain, w_in, cos, sin, glu_bias)


def _attn_kernel(lq1_ref, lk1_ref, lq2_ref, lk2_ref, g_ref, q_ref, k_ref, v_ref, o_ref,
                 s_ref, a_ref, part_ref, stat_ref, *, tq, lam_init):
    S = q_ref.shape[0]
    d = HEAD_DIM
    rc = BF16_ROWS
    nq = S // tq
    lam = (jnp.exp(jnp.sum(lq1_ref[...] * lk1_ref[...], axis=-1, keepdims=True))
           - jnp.exp(jnp.sum(lq2_ref[...] * lk2_ref[...], axis=-1, keepdims=True))
           + lam_init)
    row = lax.broadcasted_iota(jnp.int32, (tq, tq), 0)
    col = lax.broadcasted_iota(jnp.int32, (tq, tq), 1)
    causal = col <= row
    contract_last = (((1,), (1,)), ((), ()))

    def scores(i):
        off = i * tq
        end = off + tq
        sl = s_ref.at[i % 2]
        q = q_ref[off:end, :]
        for h in range(2):
            qh = q[:, h * d:(h + 1) * d]
            if off:
                sl[h, :, :off] = lax.dot_general(qh, k_ref[:off, h * d:(h + 1) * d], contract_last,
                                                 preferred_element_type=_F32)
            sd = lax.dot_general(qh, k_ref[off:end, h * d:(h + 1) * d], contract_last,
                                 preferred_element_type=_F32)
            sl[h, :, off:end] = jnp.where(causal, sd, NEG_INF)

    def output(i):
        off = i * tq
        end = off + tq
        half = tq // 2
        top = jnp.dot(a_ref[:half, :end - half], v_ref[:end - half, :], preferred_element_type=_F32)
        bot = jnp.dot(a_ref[half:, :end], v_ref[:end, :], preferred_element_type=_F32)
        for r0, o in ((off, top), (off + half, bot)):
            o = _rms(o, g_ref[...], SUBLN_EPS) * (1.0 - lam_init)
            o_ref[r0:r0 + half, :] = o.astype(_BF16)

    scores(0)
    for i in range(nq):
        off = i * tq
        end = off + tq
        ncol = end // LANES
        sl = s_ref.at[i % 2]

        def lane_block(ref, h, r0, c):
            return ref[h, pl.ds(r0, rc), c * LANES:(c + 1) * LANES]

        def segments(make_body):
            for g in range(tq // LANES):
                nvalid = off // LANES + g + 1
                lax.fori_loop(g * (LANES // rc), (g + 1) * (LANES // rc), make_body(nvalid), 0)

        def max_pass(nvalid):
            def body(r, _):
                r0 = pl.multiple_of(r * rc, rc)
                for h in range(2):
                    m = [lane_block(sl, h, r0, c) for c in range(min(2, nvalid))]
                    for c in range(2, nvalid):
                        m[c % 2] = jnp.maximum(m[c % 2], lane_block(sl, h, r0, c))
                    part_ref[h, pl.ds(r0, rc), :] = m[0] if nvalid == 1 else jnp.maximum(m[0], m[1])
                return 0
            return body

        segments(max_pass)
        for h in range(2):
            m = jnp.max(part_ref[h], axis=-1, keepdims=True)
            stat_ref[h] = jnp.broadcast_to(m, (tq, LANES))
        if i + 1 < nq:
            scores(i + 1)

        def exp_pass(nvalid):
            def body(r, _):
                r0 = pl.multiple_of(r * rc, rc)
                for h in range(2):
                    m = stat_ref[h, pl.ds(r0, rc), :]
                    acc = None
                    for c in range(nvalid):
                        p = jnp.exp2(lane_block(sl, h, r0, c) - m)
                        sl[h, pl.ds(r0, rc), c * LANES:(c + 1) * LANES] = p
                        acc = p if acc is None else acc + p
                    part_ref[h, pl.ds(r0, rc), :] = acc
                return 0
            return body

        segments(exp_pass)
        l1 = jnp.sum(part_ref[0], axis=-1, keepdims=True)
        l2 = jnp.sum(part_ref[1], axis=-1, keepdims=True)
        stat_ref[0] = jnp.broadcast_to(1.0 / l1, (tq, LANES))
        stat_ref[1] = jnp.broadcast_to(lam / l2, (tq, LANES))
        if i > 0:
            output(i - 1)

        def combine_pass(nvalid):
            def body(r, _):
                r0 = pl.multiple_of(r * rc, rc)
                c1 = stat_ref[0, pl.ds(r0, rc), :]
                c2 = stat_ref[1, pl.ds(r0, rc), :]
                for c in range(nvalid):
                    a = lane_block(sl, 0, r0, c) * c1 - lane_block(sl, 1, r0, c) * c2
                    a_ref[pl.ds(r0, rc), c * LANES:(c + 1) * LANES] = a.astype(_BF16)
                if nvalid < ncol:
                    a_ref[pl.ds(r0, rc), nvalid * LANES:end] = jnp.zeros((rc, end - nvalid * LANES), _BF16)
                return 0
            return body

        segments(combine_pass)
    output(nq - 1)


def _attention(z, lq1, lk1, lq2, lk2, subln_gain, *, batch, seq, tq, lam_init):
    T = z.shape[0]
    hw = V_HEAD_DIM
    assert seq % tq == 0 and tq % LANES == 0
    kern = functools.partial(_attn_kernel, tq=tq, lam_init=lam_init)
    vec = pl.BlockSpec((1, HEAD_DIM), lambda b, h: (0, 0))
    return pl.pallas_call(
        kern,
        grid=(batch, N_HEADS),
        in_specs=[
            vec, vec, vec, vec,
            pl.BlockSpec((1, hw), lambda b, h: (0, 0)),
            pl.BlockSpec((seq, hw), lambda b, h: (b, h)),
            pl.BlockSpec((seq, hw), lambda b, h: (b, N_HEADS + h)),
            pl.BlockSpec((seq, hw), lambda b, h: (b, 2 * N_HEADS + h)),
        ],
        out_specs=pl.BlockSpec((seq, hw), lambda b, h: (b, h)),
        out_shape=jax.ShapeDtypeStruct((T, N_HEADS * hw), _BF16),
        scratch_shapes=[pltpu.VMEM((2, 2, tq, seq), _F32), pltpu.VMEM((tq, seq), _BF16),
                        pltpu.VMEM((2, tq, LANES), _F32), pltpu.VMEM((2, tq, LANES), _F32)],
        compiler_params=_params("parallel", "parallel"),
        name="diff_attn",
    )(lq1, lk1, lq2, lk2, subln_gain, z, z, z)


def _conv_kernel(zv_ref, zg_ref, hv_ref, hg_ref, w_ref, b_ref, lg_ref, lb_ref, o_ref, g_ref, y_ref, *,
                 rc):
    s = pl.program_id(1)
    tc, C = o_ref.shape

    def glu(val, gate):
        return val.astype(_F32) * jax.nn.sigmoid(gate.astype(_F32))

    @pl.when(s == 0)
    def _():
        g_ref[:HALO, :] = jnp.zeros((HALO, C), _F32)

    @pl.when(s > 0)
    def _():
        g_ref[:HALO, :] = glu(hv_ref[...], hg_ref[...])

    g_ref[HALO:, :] = glu(zv_ref[...], zg_ref[...])

    base = HALO - (CONV_WIDTH - 1)

    def chan_body(ci, _):
        c0 = pl.multiple_of(ci * LANES, LANES)

        def row_body(ri, _):
            r0 = pl.multiple_of(ri * rc, rc)
            win = g_ref[pl.ds(r0, rc + HALO), pl.ds(c0, LANES)]
            acc = None
            for r in range(SUBLANES):
                part = None
                for k in range(CONV_WIDTH):
                    if (base + k) % SUBLANES == r:
                        term = w_ref[k:k + 1, pl.ds(c0, LANES)] * win[base + k:base + k + rc, :]
                        part = term if part is None else part + term
                acc = part if acc is None else acc + part
            y_ref[pl.ds(r0, rc), pl.ds(c0, LANES)] = acc
            return 0

        return lax.fori_loop(0, tc // rc, row_body, 0)

    lax.fori_loop(0, C // LANES, chan_body, 0)

    y = y_ref[...] + b_ref[...]
    mu = jnp.mean(y, axis=-1, keepdims=True)
    yc = y - mu
    yn = yc * lax.rsqrt(jnp.mean(yc * yc, axis=-1, keepdims=True) + LN_EPS)
    yn = yn * lg_ref[...] + lb_ref[...]
    o_ref[...] = (yn * jax.nn.sigmoid(yn)).astype(_BF16)


def _conv_branch(z, dw_kernel, dw_bias, ln_gain, ln_bias, *, batch, seq, tc, val_col, rc=64):
    T = z.shape[0]
    C = dw_bias.shape[1]
    assert seq % tc == 0 and tc % HALO == 0 and tc % rc == 0 and val_col % C == 0
    vb = val_col // C
    sblocks = seq // tc
    hpb = tc // HALO

    def halo_map(col):
        return lambda b, s: (jnp.maximum((b * sblocks + s) * hpb - 1, 0), col)

    vec = pl.BlockSpec((1, C), lambda b, s: (0, 0))
    kern = functools.partial(_conv_kernel, rc=rc)
    return pl.pallas_call(
        kern,
        grid=(batch, sblocks),
        in_specs=[
            pl.BlockSpec((tc, C), lambda b, s: (b * sblocks + s, vb)),
            pl.BlockSpec((tc, C), lambda b, s: (b * sblocks + s, vb + 1)),
            pl.BlockSpec((HALO, C), halo_map(vb)),
            pl.BlockSpec((HALO, C), halo_map(vb + 1)),
            pl.BlockSpec((CONV_WIDTH, C), lambda b, s: (0, 0)),
            vec, vec, vec,
        ],
        out_specs=pl.BlockSpec((tc, C), lambda b, s: (b * sblocks + s, 0)),
        out_shape=jax.ShapeDtypeStruct((T, C), _BF16),
        scratch_shapes=[pltpu.VMEM((tc + HALO, C), _F32), pltpu.VMEM((tc, C), _F32)],
        compiler_params=_params("parallel", "arbitrary"),
        name="conv_branch",
    )(z, z, z, z, dw_kernel, dw_bias, ln_gain, ln_bias)


def _merge_kernel(c_ref, a_ref, ga_ref, gc_ref, x_ref, wc_ref, bc_ref, wo_ref, gpost_ref, gff_ref,
                  h_ref, u_ref):
    conv_out = jnp.dot(c_ref[...], wc_ref[...], preferred_element_type=_F32) + bc_ref[...]
    m = ga_ref[...].astype(_F32) * a_ref[...].astype(_F32) + gc_ref[...].astype(_F32) * conv_out
    y = jnp.dot(m.astype(_BF16), wo_ref[...], preferred_element_type=_F32)
    h = x_ref[...] + _rms(y, gpost_ref[...], NORM_EPS)
    h_ref[...] = h
    u_ref[...] = _rms(h, gff_ref[...], NORM_EPS).astype(_BF16)


def _merge_proj(c_act, attn, z, x2, w_conv_out, b_conv_out, w_out, g_post, g_ff, *, tm, ga_col):
    T, D = x2.shape
    assert T % tm == 0 and ga_col % D == 0
    gb = ga_col // D
    row = lambda col: pl.BlockSpec((tm, D), lambda i: (i, col))
    const = lambda shape: pl.BlockSpec(shape, lambda i: (0, 0), pipeline_mode=pl.Buffered(1))
    return pl.pallas_call(
        _merge_kernel,
        grid=(T // tm,),
        in_specs=[row(0), row(0), row(gb), row(gb + 1), row(0),
                  const((D, D)), const((1, D)), const((D, D)), const((1, D)), const((1, D))],
        out_specs=[row(0), row(0)],
        out_shape=[jax.ShapeDtypeStruct((T, D), _F32), jax.ShapeDtypeStruct((T, D), _BF16)],
        compiler_params=_params("parallel"),
        name="merge_proj",
    )(c_act, attn, z, z, x2, w_conv_out, b_conv_out, w_out, g_post, g_ff)


def _ffn_kernel(u_ref, w1_ref, w2_ref, h_ref, g_ref, o_ref, acc_ref):
    j = pl.program_id(1)

    @pl.when(j == 0)
    def _():
        acc_ref[...] = jnp.zeros_like(acc_ref)

    a = jnp.dot(u_ref[...], w1_ref[...], preferred_element_type=_F32)
    a = jnp.square(jnp.maximum(a, 0.0)).astype(_BF16)
    acc_ref[...] += jnp.dot(a, w2_ref[...], preferred_element_type=_F32)

    @pl.when(j == pl.num_programs(1) - 1)
    def _():
        o_ref[...] = h_ref[...] + _rms(acc_ref[...], g_ref[...], NORM_EPS)


def _ffn(u2, w1, w2, h1, gain, *, tm, tf):
    T, D = h1.shape
    F = w1.shape[1]
    assert T % tm == 0 and F % tf == 0
    return pl.pallas_call(
        _ffn_kernel,
        grid=(T // tm, F // tf),
        in_specs=[
            pl.BlockSpec((tm, D), lambda i, j: (i, 0)),
            pl.BlockSpec((D, tf), lambda i, j: (0, j)),
            pl.BlockSpec((tf, D), lambda i, j: (j, 0)),
            pl.BlockSpec((tm, D), lambda i, j: (i, 0)),
            pl.BlockSpec((1, D), lambda i, j: (0, 0)),
        ],
        out_specs=pl.BlockSpec((tm, D), lambda i, j: (i, 0)),
        out_shape=jax.ShapeDtypeStruct((T, D), _F32),
        scratch_shapes=[pltpu.VMEM((tm, D), _F32)],
        compiler_params=_params("parallel", "arbitrary"),
        name="ffn",
    )(u2, w1, w2, h1, gain)


def _rope_tables(seq):
    d = HEAD_DIM
    inv_freq = 1.0 / (ROPE_THETA ** (jnp.arange(0, d, 2, dtype=_F32) / d))
    ang = jnp.arange(seq, dtype=_F32)[:, None] * inv_freq[None, :]
    cos = jnp.cos(ang)
    sin = jnp.sin(ang)
    return jnp.concatenate([cos, cos], axis=-1), jnp.concatenate([-sin, sin], axis=-1)


def _layer(h2d, l, p, *, batch, seq):
    D = h2d.shape[1]
    lam_init = 0.8 - 0.6 * math.exp(-0.3 * l)
    row = lambda a: a[l][None, :].astype(_F32)
    cos, sin = _rope_tables(seq)
    qk_w = N_HEADS * 2 * HEAD_DIM
    conv_col = 2 * qk_w + N_HEADS * V_HEAD_DIM
    conv_w = p['glu_bias'].shape[1]

    tm_in = min(1024, seq)
    z = _inproj(h2d, row(p['pre_mix_gain']), p['w_in'][l].astype(_BF16), cos, sin,
                row(p['glu_bias']), seq=seq, tm=tm_in, tn=1024)
    attn = _attention(z, row(p['lambda_q1']), row(p['lambda_k1']), row(p['lambda_q2']),
                      row(p['lambda_k2']), row(p['subln_gain']),
                      batch=batch, seq=seq, tq=min(512, seq), lam_init=lam_init)
    c_act = _conv_branch(z, p['dw_kernel'][l].astype(_F32), row(p['dw_bias']),
                         row(p['conv_ln_gain']), row(p['conv_ln_bias']),
                         batch=batch, seq=seq, tc=min(256, seq), val_col=conv_col)
    h1, u2 = _merge_proj(c_act, attn, z, h2d, p['w_conv_out'][l].astype(_BF16),
                         row(p['b_conv_out']), p['w_out'][l].astype(_BF16),
                         row(p['post_mix_gain']), row(p['pre_ff_gain']),
                         tm=min(256, seq), ga_col=conv_col + conv_w)
    return _ffn(u2, p['w_ff1'][l].astype(_BF16), p['w_ff2'][l].astype(_BF16), h1,
                row(p['post_ff_gain']), tm=min(512, seq), tf=1024)


def kernel(x, pre_mix_gain, w_in, lambda_q1, lambda_k1, lambda_q2, lambda_k2, subln_gain, glu_bias,
           dw_kernel, dw_bias, conv_ln_gain, conv_ln_bias, w_conv_out, b_conv_out, w_out,
           post_mix_gain, pre_ff_gain, w_ff1, w_ff2, post_ff_gain):
    B, S, D = x.shape
    p = dict(pre_mix_gain=pre_mix_gain, w_in=w_in, lambda_q1=lambda_q1, lambda_k1=lambda_k1,
             lambda_q2=lambda_q2, lambda_k2=lambda_k2, subln_gain=subln_gain, glu_bias=glu_bias,
             dw_kernel=dw_kernel, dw_bias=dw_bias, conv_ln_gain=conv_ln_gain,
             conv_ln_bias=conv_ln_bias, w_conv_out=w_conv_out, b_conv_out=b_conv_out, w_out=w_out,
             post_mix_gain=post_mix_gain, pre_ff_gain=pre_ff_gain, w_ff1=w_ff1, w_ff2=w_ff2,
             post_ff_gain=post_ff_gain)
    h = x.reshape(B * S, D).astype(_F32)
    for l in range(w_in.shape[0]):
        h = _layer(h, l, p, batch=B, seq=S)
    return h.reshape(B, S, D).astype(x.dtype)
```

```python
import functools
import math

import jax
import jax.numpy as jnp
from jax import lax
from jax.experimental import pallas as pl
from jax.experimental.pallas import tpu as pltpu

N_HEADS = 8
HEAD_DIM = 128
V_HEAD_DIM = 2 * HEAD_DIM
ROPE_THETA = 10000.0
SUBLN_EPS = 1e-5
CONV_WIDTH = 31
LN_EPS = 1e-5
NORM_EPS = 1e-6
NEG_INF = -1e30
LOG2_E = 1.4426950408889634

LANES = 128
SUBLANES = 8
BF16_ROWS = 16
VMEM_LIMIT_BYTES = 56 * 1024 * 1024
HALO = 32

_BF16 = jnp.bfloat16
_F32 = jnp.float32


def _params(*semantics):
    return pltpu.CompilerParams(dimension_semantics=semantics, vmem_limit_bytes=VMEM_LIMIT_BYTES)


def _rms(v, gain, eps):
    return v * lax.rsqrt(jnp.mean(v * v, axis=-1, keepdims=True) + eps) * gain


def _inproj_kernel(x_ref, g_ref, w_ref, cos_ref, sin_ref, b_ref, z_ref, u_ref, *,
                   tn, qk_tiles, v_tiles, conv_tiles):
    j = pl.program_id(1)

    @pl.when(j == 0)
    def _():
        u_ref[...] = _rms(x_ref[...], g_ref[...], NORM_EPS).astype(_BF16)

    def matmul():
        return jnp.dot(u_ref[...], w_ref[...], preferred_element_type=_F32)

    def rope_store(scale):
        acc = matmul()
        cos = cos_ref[...]
        sin = sin_ref[...]
        for c in range(tn // HEAD_DIM):
            sl = slice(c * HEAD_DIM, (c + 1) * HEAD_DIM)
            a = acc[:, sl]
            r = a * cos + pltpu.roll(a, HEAD_DIM // 2, axis=1) * sin
            if scale is not None:
                r = r * scale
            z_ref[:, sl] = r.astype(_BF16)

    q_end = qk_tiles
    k_end = 2 * qk_tiles
    v_end = k_end + v_tiles
    c_end = v_end + conv_tiles

    @pl.when(j < q_end)
    def _():
        rope_store(HEAD_DIM ** -0.5 * LOG2_E)

    @pl.when((j >= q_end) & (j < k_end))
    def _():
        rope_store(None)

    @pl.when((j >= k_end) & (j < v_end))
    def _():
        z_ref[...] = matmul().astype(_BF16)

    @pl.when((j >= v_end) & (j < c_end))
    def _():
        z_ref[...] = (matmul() + b_ref[...]).astype(_BF16)

    @pl.when(j >= c_end)
    def _():
        z_ref[...] = jax.nn.sigmoid(matmul()).astype(_BF16)


def _inproj(x2, gain, w_in, cos, sin, glu_bias, *, seq, tm, tn):
    T, D = x2.shape
    W = w_in.shape[1]
    qk_w = N_HEADS * 2 * HEAD_DIM
    v_w = N_HEADS * V_HEAD_DIM
    conv_w = glu_bias.shape[1]
    assert qk_w % tn == 0 and v_w % tn == 0 and conv_w % tn == 0 and W % tn == 0
    assert seq % tm == 0 and T % tm == 0
    qk_tiles, v_tiles, conv_tiles = qk_w // tn, v_w // tn, conv_w // tn
    conv0 = 2 * qk_tiles + v_tiles
    pos_blocks = seq // tm
    kern = functools.partial(_inproj_kernel, tn=tn, qk_tiles=qk_tiles, v_tiles=v_tiles,
                             conv_tiles=conv_tiles)
    return pl.pallas_call(
        kern,
        grid=(T // tm, W // tn),
        in_specs=[
            pl.BlockSpec((tm, D), lambda i, j: (i, 0)),
            pl.BlockSpec((1, D), lambda i, j: (0, 0)),
            pl.BlockSpec((D, tn), lambda i, j: (0, j)),
            pl.BlockSpec((tm, HEAD_DIM), lambda i, j: (i % pos_blocks, 0)),
            pl.BlockSpec((tm, HEAD_DIM), lambda i, j: (i % pos_blocks, 0)),
            pl.BlockSpec((1, tn), lambda i, j: (0, jnp.clip(j - conv0, 0, conv_tiles - 1))),
        ],
        out_specs=pl.BlockSpec((tm, tn), lambda i, j: (i, j)),
        out_shape=jax.ShapeDtypeStruct((T, W), _BF16),
        scratch_shapes=[pltpu.VMEM((tm, D), _BF16)],
        compiler_params=_params("parallel", "arbitrary"),
        name="inproj",
    )(x2, gain, w_in, cos, sin, glu_bias)


def _attn_kernel(lq1_ref, lk1_ref, lq2_ref, lk2_ref, g_ref, q_ref, k_ref, v_ref, o_ref,
                 s_ref, a_ref, part_ref, stat_ref, *, tq, lam_init):
    S = q_ref.shape[0]
    d = HEAD_DIM
    rc = BF16_ROWS
    nq = S // tq
    lam = (jnp.exp(jnp.sum(lq1_ref[...] * lk1_ref[...], axis=-1, keepdims=True))
           - jnp.exp(jnp.sum(lq2_ref[...] * lk2_ref[...], axis=-1, keepdims=True))
           + lam_init)
    row = lax.broadcasted_iota(jnp.int32, (tq, tq), 0)
    col = lax.broadcasted_iota(jnp.int32, (tq, tq), 1)
    causal = col <= row
    contract_last = (((1,), (1,)), ((), ()))

    def scores(i):
        off = i * tq
        end = off + tq
        sl = s_ref.at[i % 2]
        q = q_ref[off:end, :]
        for h in range(2):
            qh = q[:, h * d:(h + 1) * d]
            if off:
                sl[h, :, :off] = lax.dot_general(qh, k_ref[:off, h * d:(h + 1) * d], contract_last,
                                                 preferred_element_type=_F32)
            sd = lax.dot_general(qh, k_ref[off:end, h * d:(h + 1) * d], contract_last,
                                 preferred_element_type=_F32)
            sl[h, :, off:end] = jnp.where(causal, sd, NEG_INF)

    def output(i):
        off = i * tq
        end = off + tq
        half = tq // 2
        top = jnp.dot(a_ref[:half, :end - half], v_ref[:end - half, :], preferred_element_type=_F32)
        bot = jnp.dot(a_ref[half:, :end], v_ref[:end, :], preferred_element_type=_F32)
        for r0, o in ((off, top), (off + half, bot)):
            o = _rms(o, g_ref[...], SUBLN_EPS) * (1.0 - lam_init)
            o_ref[r0:r0 + half, :] = o.astype(_BF16)

    scores(0)
    for i in range(nq):
        off = i * tq
        end = off + tq
        ncol = end // LANES
        sl = s_ref.at[i % 2]

        def lane_block(ref, h, r0, c):
            return ref[h, pl.ds(r0, rc), c * LANES:(c + 1) * LANES]

        def segments(make_body):
            for g in range(tq // LANES):
                nvalid = off // LANES + g + 1
                body = make_body(nvalid)
                for r in range(g * (LANES // rc), (g + 1) * (LANES // rc)):
                    body(r, 0)

        def max_pass(nvalid):
            def body(r, _):
                r0 = r * rc
                for h in range(2):
                    m = [lane_block(sl, h, r0, c) for c in range(min(2, nvalid))]
                    for c in range(2, nvalid):
                        m[c % 2] = jnp.maximum(m[c % 2], lane_block(sl, h, r0, c))
                    part_ref[h, pl.ds(r0, rc), :] = m[0] if nvalid == 1 else jnp.maximum(m[0], m[1])
                return 0
            return body

        segments(max_pass)
        for h in range(2):
            m = jnp.max(part_ref[h], axis=-1, keepdims=True)
            stat_ref[h] = jnp.broadcast_to(m, (tq, LANES))
        if i + 1 < nq:
            scores(i + 1)

        def exp_pass(nvalid):
            def body(r, _):
                r0 = r * rc
                for h in range(2):
                    m = stat_ref[h, pl.ds(r0, rc), :]
                    acc = None
                    for c in range(nvalid):
                        p = jnp.exp2(lane_block(sl, h, r0, c) - m)
                        sl[h, pl.ds(r0, rc), c * LANES:(c + 1) * LANES] = p
                        acc = p if acc is None else acc + p
                    part_ref[h, pl.ds(r0, rc), :] = acc
                return 0
            return body

        segments(exp_pass)
        l1 = jnp.sum(part_ref[0], axis=-1, keepdims=True)
        l2 = jnp.sum(part_ref[1], axis=-1, keepdims=True)
        stat_ref[0] = jnp.broadcast_to(1.0 / l1, (tq, LANES))
        stat_ref[1] = jnp.broadcast_to(lam / l2, (tq, LANES))
        if i > 0:
            output(i - 1)

        def combine_pass(nvalid):
            def body(r, _):
                r0 = r * rc
                c1 = stat_ref[0, pl.ds(r0, rc), :]
                c2 = stat_ref[1, pl.ds(r0, rc), :]
                for c in range(nvalid):
                    a = lane_block(sl, 0, r0, c) * c1 - lane_block(sl, 1, r0, c) * c2
                    a_ref[pl.ds(r0, rc), c * LANES:(c + 1) * LANES] = a.astype(_BF16)
                if nvalid < ncol:
                    a_ref[pl.ds(r0, rc), nvalid * LANES:end] = jnp.zeros((rc, end - nvalid * LANES), _BF16)
                return 0
            return body

        segments(combine_pass)
    output(nq - 1)


def _attention(z, lq1, lk1, lq2, lk2, subln_gain, *, batch, seq, tq, lam_init):
    T = z.shape[0]
    hw = V_HEAD_DIM
    assert seq % tq == 0 and tq % LANES == 0
    kern = functools.partial(_attn_kernel, tq=tq, lam_init=lam_init)
    vec = pl.BlockSpec((1, HEAD_DIM), lambda b, h: (0, 0))
    return pl.pallas_call(
        kern,
        grid=(batch, N_HEADS),
        in_specs=[
            vec, vec, vec, vec,
            pl.BlockSpec((1, hw), lambda b, h: (0, 0)),
            pl.BlockSpec((seq, hw), lambda b, h: (b, h)),
            pl.BlockSpec((seq, hw), lambda b, h: (b, N_HEADS + h)),
            pl.BlockSpec((seq, hw), lambda b, h: (b, 2 * N_HEADS + h)),
        ],
        out_specs=pl.BlockSpec((seq, hw), lambda b, h: (b, h)),
        out_shape=jax.ShapeDtypeStruct((T, N_HEADS * hw), _BF16),
        scratch_shapes=[pltpu.VMEM((2, 2, tq, seq), _F32), pltpu.VMEM((tq, seq), _BF16),
                        pltpu.VMEM((2, tq, LANES), _F32), pltpu.VMEM((2, tq, LANES), _F32)],
        compiler_params=_params("parallel", "parallel"),
        name="diff_attn",
    )(lq1, lk1, lq2, lk2, subln_gain, z, z, z)


def _conv_kernel(zv_ref, zg_ref, hv_ref, hg_ref, w_ref, b_ref, lg_ref, lb_ref, o_ref, g_ref, y_ref, *,
                 rc):
    s = pl.program_id(1)
    tc, C = o_ref.shape

    def glu(val, gate):
        return val.astype(_F32) * jax.nn.sigmoid(gate.astype(_F32))

    @pl.when(s == 0)
    def _():
        g_ref[:HALO, :] = jnp.zeros((HALO, C), _F32)

    @pl.when(s > 0)
    def _():
        g_ref[:HALO, :] = glu(hv_ref[...], hg_ref[...])

    g_ref[HALO:, :] = glu(zv_ref[...], zg_ref[...])

    base = HALO - (CONV_WIDTH - 1)

    def chan_body(ci, _):
        c0 = pl.multiple_of(ci * LANES, LANES)

        def row_body(ri, _):
            r0 = pl.multiple_of(ri * rc, rc)
            win = g_ref[pl.ds(r0, rc + HALO), pl.ds(c0, LANES)]
            acc = None
            for r in range(SUBLANES):
                part = None
                for k in range(CONV_WIDTH):
                    if (base + k) % SUBLANES == r:
                        term = w_ref[k:k + 1, pl.ds(c0, LANES)] * win[base + k:base + k + rc, :]
                        part = term if part is None else part + term
                acc = part if acc is None else acc + part
            y_ref[pl.ds(r0, rc), pl.ds(c0, LANES)] = acc
            return 0

        return lax.fori_loop(0, tc // rc, row_body, 0)

    lax.fori_loop(0, C // LANES, chan_body, 0)

    y = y_ref[...] + b_ref[...]
    mu = jnp.mean(y, axis=-1, keepdims=True)
    yc = y - mu
    yn = yc * lax.rsqrt(jnp.mean(yc * yc, axis=-1, keepdims=True) + LN_EPS)
    yn = yn * lg_ref[...] + lb_ref[...]
    o_ref[...] = (yn * jax.nn.sigmoid(yn)).astype(_BF16)


def _conv_branch(z, dw_kernel, dw_bias, ln_gain, ln_bias, *, batch, seq, tc, val_col, rc=64):
    T = z.shape[0]
    C = dw_bias.shape[1]
    assert seq % tc == 0 and tc % HALO == 0 and tc % rc == 0 and val_col % C == 0
    vb = val_col // C
    sblocks = seq // tc
    hpb = tc // HALO

    def halo_map(col):
        return lambda b, s: (jnp.maximum((b * sblocks + s) * hpb - 1, 0), col)

    vec = pl.BlockSpec((1, C), lambda b, s: (0, 0))
    kern = functools.partial(_conv_kernel, rc=rc)
    return pl.pallas_call(
        kern,
        grid=(batch, sblocks),
        in_specs=[
            pl.BlockSpec((tc, C), lambda b, s: (b * sblocks + s, vb)),
            pl.BlockSpec((tc, C), lambda b, s: (b * sblocks + s, vb + 1)),
            pl.BlockSpec((HALO, C), halo_map(vb)),
            pl.BlockSpec((HALO, C), halo_map(vb + 1)),
            pl.BlockSpec((CONV_WIDTH, C), lambda b, s: (0, 0)),
            vec, vec, vec,
        ],
        out_specs=pl.BlockSpec((tc, C), lambda b, s: (b * sblocks + s, 0)),
        out_shape=jax.ShapeDtypeStruct((T, C), _BF16),
        scratch_shapes=[pltpu.VMEM((tc + HALO, C), _F32), pltpu.VMEM((tc, C), _F32)],
        compiler_params=_params("parallel", "arbitrary"),
        name="conv_branch",
    )(z, z, z, z, dw_kernel, dw_bias, ln_gain, ln_bias)


def _merge_kernel(c_ref, a_ref, ga_ref, gc_ref, x_ref, wc_ref, bc_ref, wo_ref, gpost_ref, gff_ref,
                  h_ref, u_ref):
    conv_out = jnp.dot(c_ref[...], wc_ref[...], preferred_element_type=_F32) + bc_ref[...]
    m = ga_ref[...].astype(_F32) * a_ref[...].astype(_F32) + gc_ref[...].astype(_F32) * conv_out
    y = jnp.dot(m.astype(_BF16), wo_ref[...], preferred_element_type=_F32)
    h = x_ref[...] + _rms(y, gpost_ref[...], NORM_EPS)
    h_ref[...] = h
    u_ref[...] = _rms(h, gff_ref[...], NORM_EPS).astype(_BF16)


def _merge_proj(c_act, attn, z, x2, w_conv_out, b_conv_out, w_out, g_post, g_ff, *, tm, ga_col):
    T, D = x2.shape
    assert T % tm == 0 and ga_col % D == 0
    gb = ga_col // D
    row = lambda col: pl.BlockSpec((tm, D), lambda i: (i, col))
    const = lambda shape: pl.BlockSpec(shape, lambda i: (0, 0), pipeline_mode=pl.Buffered(1))
    return pl.pallas_call(
        _merge_kernel,
        grid=(T // tm,),
        in_specs=[row(0), row(0), row(gb), row(gb + 1), row(0),
                  const((D, D)), const((1, D)), const((D, D)), const((1, D)), const((1, D))],
        out_specs=[row(0), row(0)],
        out_shape=[jax.ShapeDtypeStruct((T, D), _F32), jax.ShapeDtypeStruct((T, D), _BF16)],
        compiler_params=_params("parallel"),
        name="merge_proj",
    )(c_act, attn, z, z, x2, w_conv_out, b_conv_out, w_out, g_post, g_ff)


def _ffn_kernel(u_ref, w1_ref, w2_ref, h_ref, g_ref, o_ref, acc_ref):
    j = pl.program_id(1)

    @pl.when(j == 0)
    def _():
        acc_ref[...] = jnp.zeros_like(acc_ref)

    a = jnp.dot(u_ref[...], w1_ref[...], preferred_element_type=_F32)
    a = jnp.square(jnp.maximum(a, 0.0)).astype(_BF16)
    acc_ref[...] += jnp.dot(a, w2_ref[...], preferred_element_type=_F32)

    @pl.when(j == pl.num_programs(1) - 1)
    def _():
        o_ref[...] = h_ref[...] + _rms(acc_ref[...], g_ref[...], NORM_EPS)


def _ffn(u2, w1, w2, h1, gain, *, tm, tf):
    T, D = h1.shape
    F = w1.shape[1]
    assert T % tm == 0 and F % tf == 0
    return pl.pallas_call(
        _ffn_kernel,
        grid=(T // tm, F // tf),
        in_specs=[
            pl.BlockSpec((tm, D), lambda i, j: (i, 0)),
            pl.BlockSpec((D, tf), lambda i, j: (0, j)),
            pl.BlockSpec((tf, D), lambda i, j: (j, 0)),
            pl.BlockSpec((tm, D), lambda i, j: (i, 0)),
            pl.BlockSpec((1, D), lambda i, j: (0, 0)),
        ],
        out_specs=pl.BlockSpec((tm, D), lambda i, j: (i, 0)),
        out_shape=jax.ShapeDtypeStruct((T, D), _F32),
        scratch_shapes=[pltpu.VMEM((tm, D), _F32)],
        compiler_params=_params("parallel", "arbitrary"),
        name="ffn",
    )(u2, w1, w2, h1, gain)


def _rope_tables(seq):
    d = HEAD_DIM
    inv_freq = 1.0 / (ROPE_THETA ** (jnp.arange(0, d, 2, dtype=_F32) / d))
    ang = jnp.arange(seq, dtype=_F32)[:, None] * inv_freq[None, :]
    cos = jnp.cos(ang)
    sin = jnp.sin(ang)
    return jnp.concatenate([cos, cos], axis=-1), jnp.concatenate([-sin, sin], axis=-1)


def _layer(h2d, l, p, *, batch, seq):
    D = h2d.shape[1]
    lam_init = 0.8 - 0.6 * math.exp(-0.3 * l)
    row = lambda a: a[l][None, :].astype(_F32)
    cos, sin = _rope_tables(seq)
    qk_w = N_HEADS * 2 * HEAD_DIM
    conv_col = 2 * qk_w + N_HEADS * V_HEAD_DIM
    conv_w = p['glu_bias'].shape[1]

    tm_in = min(1024, seq)
    z = _inproj(h2d, row(p['pre_mix_gain']), p['w_in'][l].astype(_BF16), cos, sin,
                row(p['glu_bias']), seq=seq, tm=tm_in, tn=1024)
    attn = _attention(z, row(p['lambda_q1']), row(p['lambda_k1']), row(p['lambda_q2']),
                      row(p['lambda_k2']), row(p['subln_gain']),
                      batch=batch, seq=seq, tq=min(512, seq), lam_init=lam_init)
    c_act = _conv_branch(z, p['dw_kernel'][l].astype(_F32), row(p['dw_bias']),
                         row(p['conv_ln_gain']), row(p['conv_ln_bias']),
                         batch=batch, seq=seq, tc=min(256, seq), val_col=conv_col)
    h1, u2 = _merge_proj(c_act, attn, z, h2d, p['w_conv_out'][l].astype(_BF16),
                         row(p['b_conv_out']), p['w_out'][l].astype(_BF16),
                         row(p['post_mix_gain']), row(p['pre_ff_gain']),
                         tm=min(256, seq), ga_col=conv_col + conv_w)
    return _ffn(u2, p['w_ff1'][l].astype(_BF16), p['w_ff2'][l].astype(_BF16), h1,
                row(p['post_ff_gain']), tm=min(512, seq), tf=1024)


def kernel(x, pre_mix_gain, w_in, lambda_q1, lambda_k1, lambda_q2, lambda_k2, subln_gain, glu_bias,
           dw_kernel, dw_bias, conv_ln_gain, conv_ln_bias, w_conv_out, b_conv_out, w_out,
           post_mix_gain, pre_ff_gain, w_ff1, w_ff2, post_ff_gain):
    B, S, D = x.shape
    p = dict(pre_mix_gain=pre_mix_gain, w_in=w_in, lambda_q1=lambda_q1, lambda_k1=lambda_k1,
             lambda_q2=lambda_q2, lambda_k2=lambda_k2, subln_gain=subln_gain, glu_bias=glu_bias,
             dw_kernel=dw_kernel, dw_bias=dw_bias, conv_ln_gain=conv_ln_gain,
             conv_ln_bias=conv_ln_bias, w_conv_out=w_conv_out, b_conv_out=b_conv_out, w_out=w_out,
             post_mix_gain=post_mix_gain, pre_ff_gain=pre_ff_gain, w_ff1=w_ff1, w_ff2=w_ff2,
             post_ff_gain=post_ff_gain)
    h = x.reshape(B * S, D).astype(_F32)
    for l in range(w_in.shape[0]):
        h = _layer(h, l, p, batch=B, seq=S)
    return h.reshape(B, S, D).astype(x.dtype)
```

```python
import functools
import math

import jax
import jax.numpy as jnp
from jax import lax
from jax.experimental import pallas as pl
from jax.experimental.pallas import tpu as pltpu

N_HEADS = 8
HEAD_DIM = 128
V_HEAD_DIM = 2 * HEAD_DIM
ROPE_THETA = 10000.0
SUBLN_EPS = 1e-5
CONV_WIDTH = 31
LN_EPS = 1e-5
NORM_EPS = 1e-6
NEG_INF = -1e30
LOG2_E = 1.4426950408889634

LANES = 128
SUBLANES = 8
BF16_ROWS = 16
VMEM_LIMIT_BYTES = 56 * 1024 * 1024
HALO = 32

_BF16 = jnp.bfloat16
_F32 = jnp.float32


def _params(*semantics):
    return pltpu.CompilerParams(dimension_semantics=semantics, vmem_limit_bytes=VMEM_LIMIT_BYTES)


def _rms(v, gain, eps):
    return v * lax.rsqrt(jnp.mean(v * v, axis=-1, keepdims=True) + eps) * gain


def _inproj_kernel(x_ref, g_ref, w_ref, cos_ref, sin_ref, b_ref, z_ref, u_ref, *,
                   tn, qk_tiles, v_tiles, conv_tiles):
    j = pl.program_id(1)

    @pl.when(j == 0)
    def _():
        u_ref[...] = _rms(x_ref[...], g_ref[...], NORM_EPS).astype(_BF16)

    def matmul():
        return jnp.dot(u_ref[...], w_ref[...], preferred_element_type=_F32)

    def rope_store(scale):
        acc = matmul()
        cos = cos_ref[...]
        sin = sin_ref[...]
        for c in range(tn // HEAD_DIM):
            sl = slice(c * HEAD_DIM, (c + 1) * HEAD_DIM)
            a = acc[:, sl]
            r = a * cos + pltpu.roll(a, HEAD_DIM // 2, axis=1) * sin
            if scale is not None:
                r = r * scale
            z_ref[:, sl] = r.astype(_BF16)

    q_end = qk_tiles
    k_end = 2 * qk_tiles
    v_end = k_end + v_tiles
    c_end = v_end + conv_tiles

    @pl.when(j < q_end)
    def _():
        rope_store(HEAD_DIM ** -0.5 * LOG2_E)

    @pl.when((j >= q_end) & (j < k_end))
    def _():
        rope_store(None)

    @pl.when((j >= k_end) & (j < v_end))
    def _():
        z_ref[...] = matmul().astype(_BF16)

    @pl.when((j >= v_end) & (j < c_end))
    def _():
        z_ref[...] = (matmul() + b_ref[...]).astype(_BF16)

    @pl.when(j >= c_end)
    def _():
        z_ref[...] = jax.nn.sigmoid(matmul()).astype(_BF16)


def _inproj(x2, gain, w_in, cos, sin, glu_bias, *, seq, tm, tn):
    T, D = x2.shape
    W = w_in.shape[1]
    qk_w = N_HEADS * 2 * HEAD_DIM
    v_w = N_HEADS * V_HEAD_DIM
    conv_w = glu_bias.shape[1]
    assert qk_w % tn == 0 and v_w % tn == 0 and conv_w % tn == 0 and W % tn == 0
    assert seq % tm == 0 and T % tm == 0
    qk_tiles, v_tiles, conv_tiles = qk_w // tn, v_w // tn, conv_w // tn
    conv0 = 2 * qk_tiles + v_tiles
    pos_blocks = seq // tm
    kern = functools.partial(_inproj_kernel, tn=tn, qk_tiles=qk_tiles, v_tiles=v_tiles,
                             conv_tiles=conv_tiles)
    return pl.pallas_call(
        kern,
        grid=(T // tm, W // tn),
        in_specs=[
            pl.BlockSpec((tm, D), lambda i, j: (i, 0)),
            pl.BlockSpec((1, D), lambda i, j: (0, 0)),
            pl.BlockSpec((D, tn), lambda i, j: (0, j)),
            pl.BlockSpec((tm, HEAD_DIM), lambda i, j: (i % pos_blocks, 0)),
            pl.BlockSpec((tm, HEAD_DIM), lambda i, j: (i % pos_blocks, 0)),
            pl.BlockSpec((1, tn), lambda i, j: (0, jnp.clip(j - conv0, 0, conv_tiles - 1))),
        ],
        out_specs=pl.BlockSpec((tm, tn), lambda i, j: (i, j)),
        out_shape=jax.ShapeDtypeStruct((T, W), _BF16),
        scratch_shapes=[pltpu.VMEM((tm, D), _BF16)],
        compiler_params=_params("parallel", "arbitrary"),
        name="inproj",
    )(x2, gain, w_in, cos, sin, glu_bias)


def _attn_kernel(lq1_ref, lk1_ref, lq2_ref, lk2_ref, g_ref, q_ref, k_ref, v_ref, o_ref,
                 s_ref, a_ref, part_ref, stat_ref, *, tq, lam_init):
    S = q_ref.shape[0]
    d = HEAD_DIM
    rc = BF16_ROWS
    nq = S // tq
    lam = (jnp.exp(jnp.sum(lq1_ref[...] * lk1_ref[...], axis=-1, keepdims=True))
           - jnp.exp(jnp.sum(lq2_ref[...] * lk2_ref[...], axis=-1, keepdims=True))
           + lam_init)
    row = lax.broadcasted_iota(jnp.int32, (tq, tq), 0)
    col = lax.broadcasted_iota(jnp.int32, (tq, tq), 1)
    causal = col <= row
    contract_last = (((1,), (1,)), ((), ()))

    def scores(i):
        off = i * tq
        end = off + tq
        sl = s_ref.at[i % 2]
        q = q_ref[off:end, :]
        for h in range(2):
            qh = q[:, h * d:(h + 1) * d]
            if off:
                sl[h, :, :off] = lax.dot_general(qh, k_ref[:off, h * d:(h + 1) * d], contract_last,
                                                 preferred_element_type=_F32)
            sd = lax.dot_general(qh, k_ref[off:end, h * d:(h + 1) * d], contract_last,
                                 preferred_element_type=_F32)
            sl[h, :, off:end] = jnp.where(causal, sd, NEG_INF)

    def output(i):
        off = i * tq
        end = off + tq
        half = tq // 2
        top = jnp.dot(a_ref[:half, :end - half], v_ref[:end - half, :], preferred_element_type=_F32)
        bot = jnp.dot(a_ref[half:, :end], v_ref[:end, :], preferred_element_type=_F32)
        for r0, o in ((off, top), (off + half, bot)):
            o = _rms(o, g_ref[...], SUBLN_EPS) * (1.0 - lam_init)
            o_ref[r0:r0 + half, :] = o.astype(_BF16)

    scores(0)
    for i in range(nq):
        off = i * tq
        end = off + tq
        ncol = end // LANES
        sl = s_ref.at[i % 2]

        def lane_block(ref, h, r0, c):
            return ref[h, pl.ds(r0, rc), c * LANES:(c + 1) * LANES]

        def segments(make_body):
            for g in range(tq // LANES):
                nvalid = off // LANES + g + 1
                body = make_body(nvalid)
                for r in range(g * (LANES // rc), (g + 1) * (LANES // rc)):
                    body(r, 0)

        def max_pass(nvalid):
            def body(r, _):
                r0 = r * rc
                for h in range(2):
                    m = [lane_block(sl, h, r0, c) for c in range(min(2, nvalid))]
                    for c in range(2, nvalid):
                        m[c % 2] = jnp.maximum(m[c % 2], lane_block(sl, h, r0, c))
                    part_ref[h, pl.ds(r0, rc), :] = m[0] if nvalid == 1 else jnp.maximum(m[0], m[1])
                return 0
            return body

        segments(max_pass)
        for h in range(2):
            m = jnp.max(part_ref[h], axis=-1, keepdims=True)
            stat_ref[h] = jnp.broadcast_to(m, (tq, LANES))
        if i + 1 < nq:
            scores(i + 1)

        def exp_pass(nvalid):
            def body(r, _):
                r0 = r * rc
                for h in range(2):
                    m = stat_ref[h, pl.ds(r0, rc), :]
                    acc = None
                    for c in range(nvalid):
                        p = jnp.exp2(lane_block(sl, h, r0, c) - m)
                        sl[h, pl.ds(r0, rc), c * LANES:(c + 1) * LANES] = p
                        acc = p if acc is None else acc + p
                    part_ref[h, pl.ds(r0, rc), :] = acc
                return 0
            return body

        segments(exp_pass)
        l1 = jnp.sum(part_ref[0], axis=-1, keepdims=True)
        l2 = jnp.sum(part_ref[1], axis=-1, keepdims=True)
        stat_ref[0] = jnp.broadcast_to(1.0 / l1, (tq, LANES))
        stat_ref[1] = jnp.broadcast_to(lam / l2, (tq, LANES))
        if i > 0:
            output(i - 1)

        def combine_pass(nvalid):
            def body(r, _):
                r0 = r * rc
                c1 = stat_ref[0, pl.ds(r0, rc), :]
                c2 = stat_ref[1, pl.ds(r0, rc), :]
                for c in range(nvalid):
                    a = lane_block(sl, 0, r0, c) * c1 - lane_block(sl, 1, r0, c) * c2
                    a_ref[pl.ds(r0, rc), c * LANES:(c + 1) * LANES] = a.astype(_BF16)
                if nvalid < ncol:
                    a_ref[pl.ds(r0, rc), nvalid * LANES:end] = jnp.zeros((rc, end - nvalid * LANES), _BF16)
                return 0
            return body

        segments(combine_pass)
    output(nq - 1)


def _attention(z, lq1, lk1, lq2, lk2, subln_gain, *, batch, seq, tq, lam_init):
    T = z.shape[0]
    hw = V_HEAD_DIM
    assert seq % tq == 0 and tq % LANES == 0
    kern = functools.partial(_attn_kernel, tq=tq, lam_init=lam_init)
    vec = pl.BlockSpec((1, HEAD_DIM), lambda b, h: (0, 0))
    return pl.pallas_call(
        kern,
        grid=(batch, N_HEADS),
        in_specs=[
            vec, vec, vec, vec,
            pl.BlockSpec((1, hw), lambda b, h: (0, 0)),
            pl.BlockSpec((seq, hw), lambda b, h: (b, h)),
            pl.BlockSpec((seq, hw), lambda b, h: (b, N_HEADS + h)),
            pl.BlockSpec((seq, hw), lambda b, h: (b, 2 * N_HEADS + h)),
        ],
        out_specs=pl.BlockSpec((seq, hw), lambda b, h: (b, h)),
        out_shape=jax.ShapeDtypeStruct((T, N_HEADS * hw), _BF16),
        scratch_shapes=[pltpu.VMEM((2, 2, tq, seq), _F32), pltpu.VMEM((tq, seq), _BF16),
                        pltpu.VMEM((2, tq, LANES), _F32), pltpu.VMEM((2, tq, LANES), _F32)],
        compiler_params=_params("parallel", "parallel"),
        name="diff_attn",
    )(lq1, lk1, lq2, lk2, subln_gain, z, z, z)


def _conv_merge_kernel(zv_ref, zg_ref, hv_ref, hg_ref, w_ref, b_ref, lg_ref, lb_ref,
                       a_ref, ga_ref, gc_ref, x_ref, wc_ref, bc_ref, wo_ref, gpost_ref, gff_ref,
                       h_ref, u_ref, g_scr, y_scr, c_scr, m_scr, yo_scr, *, rc, pw, sblocks, nblk):
    t = pl.program_id(0)
    tc, C = c_scr.shape
    D = yo_scr.shape[1]

    @pl.when(t == 0)
    def _():
        c_scr[...] = jnp.zeros_like(c_scr)

    nchunk = D // pw

    def merge_chunk(n):
        cols = slice(n * pw, (n + 1) * pw)
        conv_out = jnp.dot(c_scr[...], wc_ref[:, cols], preferred_element_type=_F32) + bc_ref[:, cols]
        m = (ga_ref[:, cols].astype(_F32) * a_ref[:, cols].astype(_F32)
             + gc_ref[:, cols].astype(_F32) * conv_out)
        m_scr[:, cols] = m.astype(_BF16)

    def out_chunk(n):
        cols = slice(n * pw, (n + 1) * pw)
        yo_scr[:, cols] = jnp.dot(m_scr[...], wo_ref[:, cols], preferred_element_type=_F32)

    pieces = ([(merge_chunk, n, c_scr) for n in range(nchunk)]
              + [(out_chunk, n, m_scr) for n in range(nchunk)])

    def pace(operand_ref, produced):
        bits = pltpu.bitcast(produced[:BF16_ROWS, :], jnp.uint32)
        zero = pltpu.bitcast(lax.shift_right_logical(lax.shift_right_logical(bits, jnp.uint32(31)),
                                                     jnp.uint32(1)), _F32)
        tile = operand_ref[:BF16_ROWS, :LANES]
        operand_ref[:BF16_ROWS, :LANES] = tile + zero.astype(tile.dtype)

    def glu(val, gate):
        return val.astype(_F32) * jax.nn.sigmoid(gate.astype(_F32))

    seq_start = jnp.minimum(t, nblk - 1) % sblocks == 0
    g_scr[:HALO, :] = jnp.where(seq_start, 0.0, glu(hv_ref[...], hg_ref[...]))
    g_scr[HALO:, :] = glu(zv_ref[...], zg_ref[...])

    base = HALO - (CONV_WIDTH - 1)
    nchan = C // LANES
    stride = nchan // len(pieces)
    assert stride >= 1
    for ci in range(nchan):
        c0 = ci * LANES
        for r0 in range(0, tc, rc):
            win = g_scr[r0:r0 + rc + HALO, c0:c0 + LANES]
            acc = None
            for r in range(SUBLANES):
                part = None
                for k in range(CONV_WIDTH):
                    if (base + k) % SUBLANES == r:
                        term = w_ref[k:k + 1, c0:c0 + LANES] * win[base + k:base + k + rc, :]
                        part = term if part is None else part + term
                acc = part if acc is None else acc + part
            y_scr[r0:r0 + rc, c0:c0 + LANES] = acc
            if ci % stride == 0:
                for s0 in range(0, rc, BF16_ROWS):
                    done = acc[s0:s0 + BF16_ROWS, :] if r0 == 0 and s0 == 0 else done + acc[s0:s0 + BF16_ROWS, :]
        if ci % stride == 0 and ci // stride < len(pieces):
            fn, n, operand_ref = pieces[ci // stride]
            pace(operand_ref, done)
            fn(n)

    h = x_ref[...] + _rms(yo_scr[...], gpost_ref[...], NORM_EPS)
    h_ref[...] = h
    u_ref[...] = _rms(h, gff_ref[...], NORM_EPS).astype(_BF16)

    yb = y_scr[...] + b_ref[...]
    mu = jnp.mean(yb, axis=-1, keepdims=True)
    yc = yb - mu
    yn = yc * lax.rsqrt(jnp.mean(yc * yc, axis=-1, keepdims=True) + LN_EPS)
    yn = yn * lg_ref[...] + lb_ref[...]
    c_scr[...] = (yn * jax.nn.sigmoid(yn)).astype(_BF16)


def _conv_merge(z, attn, x2, dw_kernel, dw_bias, ln_gain, ln_bias, w_conv_out, b_conv_out, w_out,
                g_post, g_ff, *, seq, tc, val_col, ga_col, rc=64, pw=512):
    T, D = x2.shape
    C = dw_bias.shape[1]
    assert seq % tc == 0 and T % tc == 0 and tc % HALO == 0 and tc % rc == 0
    assert val_col % C == 0 and ga_col % D == 0 and C == D and D % pw == 0
    vb = val_col // C
    gb = ga_col // D
    sblocks = seq // tc
    nblk = T // tc
    hpb = tc // HALO

    conv_blk = lambda t: jnp.minimum(t, nblk - 1)
    proj_blk = lambda t: jnp.maximum(t - 1, 0)
    conv_row = lambda col: pl.BlockSpec((tc, C), lambda t: (conv_blk(t), col))
    halo = lambda col: pl.BlockSpec((HALO, C), lambda t: (jnp.maximum(conv_blk(t) * hpb - 1, 0), col))
    proj_row = lambda col: pl.BlockSpec((tc, D), lambda t: (proj_blk(t), col))
    const = lambda shape: pl.BlockSpec(shape, lambda t: (0, 0), pipeline_mode=pl.Buffered(1))
    kern = functools.partial(_conv_merge_kernel, rc=rc, pw=pw, sblocks=sblocks, nblk=nblk)
    return pl.pallas_call(
        kern,
        grid=(nblk + 1,),
        in_specs=[conv_row(vb), conv_row(vb + 1), halo(vb), halo(vb + 1),
                  const((CONV_WIDTH, C)), const((1, C)), const((1, C)), const((1, C)),
                  proj_row(0), proj_row(gb), proj_row(gb + 1), proj_row(0),
                  const((C, D)), const((1, D)), const((D, D)), const((1, D)), const((1, D))],
        out_specs=[proj_row(0), proj_row(0)],
        out_shape=[jax.ShapeDtypeStruct((T, D), _F32), jax.ShapeDtypeStruct((T, D), _BF16)],
        scratch_shapes=[pltpu.VMEM((tc + HALO, C), _F32), pltpu.VMEM((tc, C), _F32),
                        pltpu.VMEM((tc, C), _BF16), pltpu.VMEM((tc, D), _BF16),
                        pltpu.VMEM((tc, D), _F32)],
        compiler_params=_params("arbitrary"),
        name="conv_merge",
    )(z, z, z, z, dw_kernel, dw_bias, ln_gain, ln_bias, attn, z, z, x2,
      w_conv_out, b_conv_out, w_out, g_post, g_ff)


def _ffn_kernel(u_ref, w1_ref, w2_ref, h_ref, g_ref, o_ref, acc_ref):
    j = pl.program_id(1)

    @pl.when(j == 0)
    def _():
        acc_ref[...] = jnp.zeros_like(acc_ref)

    a = jnp.dot(u_ref[...], w1_ref[...], preferred_element_type=_F32)
    a = jnp.square(jnp.maximum(a, 0.0)).astype(_BF16)
    acc_ref[...] += jnp.dot(a, w2_ref[...], preferred_element_type=_F32)

    @pl.when(j == pl.num_programs(1) - 1)
    def _():
        o_ref[...] = h_ref[...] + _rms(acc_ref[...], g_ref[...], NORM_EPS)


def _ffn(u2, w1, w2, h1, gain, *, tm, tf):
    T, D = h1.shape
    F = w1.shape[1]
    assert T % tm == 0 and F % tf == 0
    return pl.pallas_call(
        _ffn_kernel,
        grid=(T // tm, F // tf),
        in_specs=[
            pl.BlockSpec((tm, D), lambda i, j: (i, 0)),
            pl.BlockSpec((D, tf), lambda i, j: (0, j)),
            pl.BlockSpec((tf, D), lambda i, j: (j, 0)),
            pl.BlockSpec((tm, D), lambda i, j: (i, 0)),
            pl.BlockSpec((1, D), lambda i, j: (0, 0)),
        ],
        out_specs=pl.BlockSpec((tm, D), lambda i, j: (i, 0)),
        out_shape=jax.ShapeDtypeStruct((T, D), _F32),
        scratch_shapes=[pltpu.VMEM((tm, D), _F32)],
        compiler_params=_params("parallel", "arbitrary"),
        name="ffn",
    )(u2, w1, w2, h1, gain)


def _rope_tables(seq):
    d = HEAD_DIM
    inv_freq = 1.0 / (ROPE_THETA ** (jnp.arange(0, d, 2, dtype=_F32) / d))
    ang = jnp.arange(seq, dtype=_F32)[:, None] * inv_freq[None, :]
    cos = jnp.cos(ang)
    sin = jnp.sin(ang)
    return jnp.concatenate([cos, cos], axis=-1), jnp.concatenate([-sin, sin], axis=-1)


def _layer(h2d, l, p, *, batch, seq):
    D = h2d.shape[1]
    lam_init = 0.8 - 0.6 * math.exp(-0.3 * l)
    row = lambda a: a[l][None, :].astype(_F32)
    cos, sin = _rope_tables(seq)
    qk_w = N_HEADS * 2 * HEAD_DIM
    conv_col = 2 * qk_w + N_HEADS * V_HEAD_DIM
    conv_w = p['glu_bias'].shape[1]

    tm_in = min(1024, seq)
    z = _inproj(h2d, row(p['pre_mix_gain']), p['w_in'][l].astype(_BF16), cos, sin,
                row(p['glu_bias']), seq=seq, tm=tm_in, tn=1024)
    attn = _attention(z, row(p['lambda_q1']), row(p['lambda_k1']), row(p['lambda_q2']),
                      row(p['lambda_k2']), row(p['subln_gain']),
                      batch=batch, seq=seq, tq=min(512, seq), lam_init=lam_init)
    h1, u2 = _conv_merge(z, attn, h2d, p['dw_kernel'][l].astype(_F32), row(p['dw_bias']),
                         row(p['conv_ln_gain']), row(p['conv_ln_bias']),
                         p['w_conv_out'][l].astype(_BF16), row(p['b_conv_out']),
                         p['w_out'][l].astype(_BF16), row(p['post_mix_gain']),
                         row(p['pre_ff_gain']), seq=seq, tc=min(256, seq), val_col=conv_col,
                         ga_col=conv_col + conv_w)
    return _ffn(u2, p['w_ff1'][l].astype(_BF16), p['w_ff2'][l].astype(_BF16), h1,
                row(p['post_ff_gain']), tm=min(512, seq), tf=1024)


def kernel(x, pre_mix_gain, w_in, lambda_q1, lambda_k1, lambda_q2, lambda_k2, subln_gain, glu_bias,
           dw_kernel, dw_bias, conv_ln_gain, conv_ln_bias, w_conv_out, b_conv_out, w_out,
           post_mix_gain, pre_ff_gain, w_ff1, w_ff2, post_ff_gain):
    B, S, D = x.shape
    p = dict(pre_mix_gain=pre_mix_gain, w_in=w_in, lambda_q1=lambda_q1, lambda_k1=lambda_k1,
             lambda_q2=lambda_q2, lambda_k2=lambda_k2, subln_gain=subln_gain, glu_bias=glu_bias,
             dw_kernel=dw_kernel, dw_bias=dw_bias, conv_ln_gain=conv_ln_gain,
             conv_ln_bias=conv_ln_bias, w_conv_out=w_conv_out, b_conv_out=b_conv_out, w_out=w_out,
             post_mix_gain=post_mix_gain, pre_ff_gain=pre_ff_gain, w_ff1=w_ff1, w_ff2=w_ff2,
             post_ff_gain=post_ff_gain)
    h = x.reshape(B * S, D).astype(_F32)
    for l in range(w_in.shape[0]):
        h = _layer(h, l, p, batch=B, seq=S)
    return h.reshape(B, S, D).astype(x.dtype)
```

```python
import functools
import math

import jax
import jax.numpy as jnp
from jax import lax
from jax.experimental import pallas as pl
from jax.experimental.pallas import tpu as pltpu

N_HEADS = 8
HEAD_DIM = 128
V_HEAD_DIM = 2 * HEAD_DIM
ROPE_THETA = 10000.0
SUBLN_EPS = 1e-5
CONV_WIDTH = 31
LN_EPS = 1e-5
NORM_EPS = 1e-6
NEG_INF = -1e30
LOG2_E = 1.4426950408889634

LANES = 128
SUBLANES = 8
BF16_ROWS = 16
VMEM_LIMIT_BYTES = 56 * 1024 * 1024
HALO = 32

_BF16 = jnp.bfloat16
_F32 = jnp.float32


def _params(*semantics):
    return pltpu.CompilerParams(dimension_semantics=semantics, vmem_limit_bytes=VMEM_LIMIT_BYTES)


def _rms(v, gain, eps):
    return v * lax.rsqrt(jnp.mean(v * v, axis=-1, keepdims=True) + eps) * gain


def _inproj_kernel(x_ref, g_ref, w_ref, cos_ref, sin_ref, b_ref, *rest,
                   tn, qk_tiles, v_tiles, conv_tiles, n_side):
    side_in = rest[:n_side]
    z_ref = rest[n_side]
    side_out = rest[n_side + 1:2 * n_side + 1]
    u_ref = rest[2 * n_side + 1]
    j = pl.program_id(1)

    @pl.when(j == 0)
    def _():
        u_ref[...] = _rms(x_ref[...], g_ref[...], NORM_EPS).astype(_BF16)

    def matmul():
        for src, dst in zip(side_in, side_out):
            dst[...] = src[...].astype(_BF16)
        return jnp.dot(u_ref[...], w_ref[...], preferred_element_type=_F32)

    def rope_store(scale):
        acc = matmul()
        cos = cos_ref[...]
        sin = sin_ref[...]
        for c in range(tn // HEAD_DIM):
            sl = slice(c * HEAD_DIM, (c + 1) * HEAD_DIM)
            a = acc[:, sl]
            r = a * cos + pltpu.roll(a, HEAD_DIM // 2, axis=1) * sin
            if scale is not None:
                r = r * scale
            z_ref[:, sl] = r.astype(_BF16)

    q_end = qk_tiles
    k_end = 2 * qk_tiles
    v_end = k_end + v_tiles
    c_end = v_end + conv_tiles

    @pl.when(j < q_end)
    def _():
        rope_store(HEAD_DIM ** -0.5 * LOG2_E)

    @pl.when((j >= q_end) & (j < k_end))
    def _():
        rope_store(None)

    @pl.when((j >= k_end) & (j < v_end))
    def _():
        z_ref[...] = matmul().astype(_BF16)

    @pl.when((j >= v_end) & (j < c_end))
    def _():
        z_ref[...] = (matmul() + b_ref[...]).astype(_BF16)

    @pl.when(j >= c_end)
    def _():
        z_ref[...] = (0.5 * jnp.tanh(0.5 * matmul()) + 0.5).astype(_BF16)


def _inproj(x2, gain, w_in, cos, sin, glu_bias, later_weights, *, seq, tm, tn):
    T, D = x2.shape
    W = w_in.shape[1]
    qk_w = N_HEADS * 2 * HEAD_DIM
    v_w = N_HEADS * V_HEAD_DIM
    conv_w = glu_bias.shape[1]
    assert qk_w % tn == 0 and v_w % tn == 0 and conv_w % tn == 0 and W % tn == 0
    assert seq % tm == 0 and T % tm == 0
    qk_tiles, v_tiles, conv_tiles = qk_w // tn, v_w // tn, conv_w // tn
    conv0 = 2 * qk_tiles + v_tiles
    pos_blocks = seq // tm
    ni, nj = T // tm, W // tn
    nchunks = 1 << ((ni * nj).bit_length() - 1)
    for w in later_weights:
        nchunks = min(nchunks, w.shape[0] // BF16_ROWS)
    chunk_map = lambda i, j: (jnp.minimum(i * nj + j, nchunks - 1), 0)
    side_specs = []
    for w in later_weights:
        assert w.shape[0] % (nchunks * BF16_ROWS) == 0
        side_specs.append(pl.BlockSpec((w.shape[0] // nchunks, w.shape[1]), chunk_map))
    kern = functools.partial(_inproj_kernel, tn=tn, qk_tiles=qk_tiles, v_tiles=v_tiles,
                             conv_tiles=conv_tiles, n_side=len(later_weights))
    outs = pl.pallas_call(
        kern,
        grid=(ni, nj),
        in_specs=[
            pl.BlockSpec((tm, D), lambda i, j: (i, 0)),
            pl.BlockSpec((1, D), lambda i, j: (0, 0)),
            pl.BlockSpec((D, tn), lambda i, j: (0, j)),
            pl.BlockSpec((tm, HEAD_DIM), lambda i, j: (i % pos_blocks, 0)),
            pl.BlockSpec((tm, HEAD_DIM), lambda i, j: (i % pos_blocks, 0)),
            pl.BlockSpec((1, tn), lambda i, j: (0, jnp.clip(j - conv0, 0, conv_tiles - 1))),
        ] + side_specs,
        out_specs=[pl.BlockSpec((tm, tn), lambda i, j: (i, j))] + side_specs,
        out_shape=[jax.ShapeDtypeStruct((T, W), _BF16)]
        + [jax.ShapeDtypeStruct(w.shape, _BF16) for w in later_weights],
        scratch_shapes=[pltpu.VMEM((tm, D), _BF16)],
        compiler_params=_params("arbitrary", "arbitrary"),
        name="inproj",
    )(x2, gain, w_in, cos, sin, glu_bias, *later_weights)
    return outs[0], outs[1:]


def _attn_kernel(lq1_ref, lk1_ref, lq2_ref, lk2_ref, g_ref, q_ref, k_ref, v_ref, o_ref,
                 s_ref, a_ref, part_ref, stat_ref, *, tq, lam_init):
    S = q_ref.shape[0]
    d = HEAD_DIM
    rc = BF16_ROWS
    nq = S // tq
    lam = (jnp.exp(jnp.sum(lq1_ref[...] * lk1_ref[...], axis=-1, keepdims=True))
           - jnp.exp(jnp.sum(lq2_ref[...] * lk2_ref[...], axis=-1, keepdims=True))
           + lam_init)
    row = lax.broadcasted_iota(jnp.int32, (tq, tq), 0)
    col = lax.broadcasted_iota(jnp.int32, (tq, tq), 1)
    causal = col <= row
    contract_last = (((1,), (1,)), ((), ()))

    def scores(i):
        off = i * tq
        end = off + tq
        sl = s_ref.at[i % 2]
        q = q_ref[off:end, :]
        for h in range(2):
            qh = q[:, h * d:(h + 1) * d]
            if off:
                sl[h, :, :off] = lax.dot_general(qh, k_ref[:off, h * d:(h + 1) * d], contract_last,
                                                 preferred_element_type=_F32)
            sd = lax.dot_general(qh, k_ref[off:end, h * d:(h + 1) * d], contract_last,
                                 preferred_element_type=_F32)
            sl[h, :, off:end] = jnp.where(causal, sd, NEG_INF)

    def output(i):
        off = i * tq
        end = off + tq
        half = tq // 2
        top = jnp.dot(a_ref[:half, :end - half], v_ref[:end - half, :], preferred_element_type=_F32)
        bot = jnp.dot(a_ref[half:, :end], v_ref[:end, :], preferred_element_type=_F32)
        for r0, o in ((off, top), (off + half, bot)):
            o = _rms(o, g_ref[...], SUBLN_EPS) * (1.0 - lam_init)
            o_ref[r0:r0 + half, :] = o.astype(_BF16)

    scores(0)
    for i in range(nq):
        off = i * tq
        end = off + tq
        ncol = end // LANES
        sl = s_ref.at[i % 2]

        def lane_block(ref, h, r0, c):
            return ref[h, pl.ds(r0, rc), c * LANES:(c + 1) * LANES]

        def segments(make_body):
            for g in range(tq // LANES):
                nvalid = off // LANES + g + 1
                body = make_body(nvalid)
                for r in range(g * (LANES // rc), (g + 1) * (LANES // rc)):
                    body(r, 0)

        def max_pass(nvalid):
            def body(r, _):
                r0 = r * rc
                for h in range(2):
                    m = [lane_block(sl, h, r0, c) for c in range(min(2, nvalid))]
                    for c in range(2, nvalid):
                        m[c % 2] = jnp.maximum(m[c % 2], lane_block(sl, h, r0, c))
                    part_ref[h, pl.ds(r0, rc), :] = m[0] if nvalid == 1 else jnp.maximum(m[0], m[1])
                return 0
            return body

        segments(max_pass)
        for h in range(2):
            m = jnp.max(part_ref[h], axis=-1, keepdims=True)
            stat_ref[h] = jnp.broadcast_to(m, (tq, LANES))
        if i + 1 < nq:
            scores(i + 1)

        def exp_pass(nvalid):
            def body(r, _):
                r0 = r * rc
                for h in range(2):
                    m = stat_ref[h, pl.ds(r0, rc), :]
                    acc = None
                    for c in range(nvalid):
                        p = jnp.exp2(lane_block(sl, h, r0, c) - m)
                        sl[h, pl.ds(r0, rc), c * LANES:(c + 1) * LANES] = p
                        acc = p if acc is None else acc + p
                    part_ref[h, pl.ds(r0, rc), :] = acc
                return 0
            return body

        segments(exp_pass)
        l1 = jnp.sum(part_ref[0], axis=-1, keepdims=True)
        l2 = jnp.sum(part_ref[1], axis=-1, keepdims=True)
        stat_ref[0] = jnp.broadcast_to(1.0 / l1, (tq, LANES))
        stat_ref[1] = jnp.broadcast_to(lam / l2, (tq, LANES))
        if i > 0:
            output(i - 1)

        def combine_pass(nvalid):
            def body(r, _):
                r0 = r * rc
                c1 = stat_ref[0, pl.ds(r0, rc), :]
                c2 = stat_ref[1, pl.ds(r0, rc), :]
                for c in range(nvalid):
                    a = lane_block(sl, 0, r0, c) * c1 - lane_block(sl, 1, r0, c) * c2
                    a_ref[pl.ds(r0, rc), c * LANES:(c + 1) * LANES] = a.astype(_BF16)
                if nvalid < ncol:
                    a_ref[pl.ds(r0, rc), nvalid * LANES:end] = jnp.zeros((rc, end - nvalid * LANES), _BF16)
                return 0
            return body

        segments(combine_pass)
    output(nq - 1)


def _attention(z, lq1, lk1, lq2, lk2, subln_gain, *, batch, seq, tq, lam_init):
    T = z.shape[0]
    hw = V_HEAD_DIM
    assert seq % tq == 0 and tq % LANES == 0
    kern = functools.partial(_attn_kernel, tq=tq, lam_init=lam_init)
    vec = pl.BlockSpec((1, HEAD_DIM), lambda b, h: (0, 0))
    return pl.pallas_call(
        kern,
        grid=(batch, N_HEADS),
        in_specs=[
            vec, vec, vec, vec,
            pl.BlockSpec((1, hw), lambda b, h: (0, 0)),
            pl.BlockSpec((seq, hw), lambda b, h: (b, h)),
            pl.BlockSpec((seq, hw), lambda b, h: (b, N_HEADS + h)),
            pl.BlockSpec((seq, hw), lambda b, h: (b, 2 * N_HEADS + h)),
        ],
        out_specs=pl.BlockSpec((seq, hw), lambda b, h: (b, h)),
        out_shape=jax.ShapeDtypeStruct((T, N_HEADS * hw), _BF16),
        scratch_shapes=[pltpu.VMEM((2, 2, tq, seq), _F32), pltpu.VMEM((tq, seq), _BF16),
                        pltpu.VMEM((2, tq, LANES), _F32), pltpu.VMEM((2, tq, LANES), _F32)],
        compiler_params=_params("parallel", "parallel"),
        name="diff_attn",
    )(lq1, lk1, lq2, lk2, subln_gain, z, z, z)


def _conv_merge_kernel(zv_ref, zg_ref, hv_ref, hg_ref, w_ref, b_ref, lg_ref, lb_ref,
                       a_ref, ga_ref, gc_ref, x_ref, wc_ref, bc_ref, wo_ref, gpost_ref, gff_ref,
                       h_ref, u_ref, g_scr, y_scr, c_scr, m_scr, yo_scr, *, rc, pw, sblocks, nblk):
    t = pl.program_id(0)
    tc, C = c_scr.shape
    D = yo_scr.shape[1]

    @pl.when(t == 0)
    def _():
        c_scr[...] = jnp.zeros_like(c_scr)

    nchunk = D // pw

    def merge_chunk(n):
        cols = slice(n * pw, (n + 1) * pw)
        conv_out = jnp.dot(c_scr[...], wc_ref[:, cols], preferred_element_type=_F32) + bc_ref[:, cols]
        m = (ga_ref[:, cols].astype(_F32) * a_ref[:, cols].astype(_F32)
             + gc_ref[:, cols].astype(_F32) * conv_out)
        m_scr[:, cols] = m.astype(_BF16)

    def out_chunk(n):
        cols = slice(n * pw, (n + 1) * pw)
        yo_scr[:, cols] = jnp.dot(m_scr[...], wo_ref[:, cols], preferred_element_type=_F32)

    pieces = ([(merge_chunk, n, c_scr) for n in range(nchunk)]
              + [(out_chunk, n, m_scr) for n in range(nchunk)])

    def pace(operand_ref, produced):
        bits = pltpu.bitcast(produced[:BF16_ROWS, :], jnp.uint32)
        zero = pltpu.bitcast(lax.shift_right_logical(lax.shift_right_logical(bits, jnp.uint32(31)),
                                                     jnp.uint32(1)), _F32)
        tile = operand_ref[:BF16_ROWS, :LANES]
        operand_ref[:BF16_ROWS, :LANES] = tile + zero.astype(tile.dtype)

    def glu(val, gate):
        return val.astype(_F32) * jax.nn.sigmoid(gate.astype(_F32))

    seq_start = jnp.minimum(t, nblk - 1) % sblocks == 0
    g_scr[:HALO, :] = jnp.where(seq_start, 0.0, glu(hv_ref[...], hg_ref[...]))
    g_scr[HALO:, :] = glu(zv_ref[...], zg_ref[...])

    base = HALO - (CONV_WIDTH - 1)
    nchan = C // LANES
    stride = nchan // len(pieces)
    assert stride >= 1
    for ci in range(nchan):
        c0 = ci * LANES
        for r0 in range(0, tc, rc):
            win = g_scr[r0:r0 + rc + HALO, c0:c0 + LANES]
            acc = None
            for r in range(SUBLANES):
                part = None
                for k in range(CONV_WIDTH):
                    if (base + k) % SUBLANES == r:
                        term = w_ref[k:k + 1, c0:c0 + LANES] * win[base + k:base + k + rc, :]
                        part = term if part is None else part + term
                acc = part if acc is None else acc + part
            y_scr[r0:r0 + rc, c0:c0 + LANES] = acc
            if ci % stride == 0:
                for s0 in range(0, rc, BF16_ROWS):
                    done = acc[s0:s0 + BF16_ROWS, :] if r0 == 0 and s0 == 0 else done + acc[s0:s0 + BF16_ROWS, :]
        if ci % stride == 0 and ci // stride < len(pieces):
            fn, n, operand_ref = pieces[ci // stride]
            pace(operand_ref, done)
            fn(n)

    h = x_ref[...] + _rms(yo_scr[...], gpost_ref[...], NORM_EPS)
    h_ref[...] = h
    u_ref[...] = _rms(h, gff_ref[...], NORM_EPS).astype(_BF16)

    yb = y_scr[...] + b_ref[...]
    mu = jnp.mean(yb, axis=-1, keepdims=True)
    yc = yb - mu
    yn = yc * lax.rsqrt(jnp.mean(yc * yc, axis=-1, keepdims=True) + LN_EPS)
    yn = yn * lg_ref[...] + lb_ref[...]
    c_scr[...] = (yn * jax.nn.sigmoid(yn)).astype(_BF16)


def _conv_merge(z, attn, x2, dw_kernel, dw_bias, ln_gain, ln_bias, w_conv_out, b_conv_out, w_out,
                g_post, g_ff, *, seq, tc, val_col, ga_col, rc=64, pw=512):
    T, D = x2.shape
    C = dw_bias.shape[1]
    assert seq % tc == 0 and T % tc == 0 and tc % HALO == 0 and tc % rc == 0
    assert val_col % C == 0 and ga_col % D == 0 and C == D and D % pw == 0
    vb = val_col // C
    gb = ga_col // D
    sblocks = seq // tc
    nblk = T // tc
    hpb = tc // HALO

    conv_blk = lambda t: jnp.minimum(t, nblk - 1)
    proj_blk = lambda t: jnp.maximum(t - 1, 0)
    conv_row = lambda col: pl.BlockSpec((tc, C), lambda t: (conv_blk(t), col))
    halo = lambda col: pl.BlockSpec((HALO, C), lambda t: (jnp.maximum(conv_blk(t) * hpb - 1, 0), col))
    proj_row = lambda col: pl.BlockSpec((tc, D), lambda t: (proj_blk(t), col))
    const = lambda shape: pl.BlockSpec(shape, lambda t: (0, 0), pipeline_mode=pl.Buffered(1))
    kern = functools.partial(_conv_merge_kernel, rc=rc, pw=pw, sblocks=sblocks, nblk=nblk)
    return pl.pallas_call(
        kern,
        grid=(nblk + 1,),
        in_specs=[conv_row(vb), conv_row(vb + 1), halo(vb), halo(vb + 1),
                  const((CONV_WIDTH, C)), const((1, C)), const((1, C)), const((1, C)),
                  proj_row(0), proj_row(gb), proj_row(gb + 1), proj_row(0),
                  const((C, D)), const((1, D)), const((D, D)), const((1, D)), const((1, D))],
        out_specs=[proj_row(0), proj_row(0)],
        out_shape=[jax.ShapeDtypeStruct((T, D), _F32), jax.ShapeDtypeStruct((T, D), _BF16)],
        scratch_shapes=[pltpu.VMEM((tc + HALO, C), _F32), pltpu.VMEM((tc, C), _F32),
                        pltpu.VMEM((tc, C), _BF16), pltpu.VMEM((tc, D), _BF16),
                        pltpu.VMEM((tc, D), _F32)],
        compiler_params=_params("arbitrary"),
        name="conv_merge",
    )(z, z, z, z, dw_kernel, dw_bias, ln_gain, ln_bias, attn, z, z, x2,
      w_conv_out, b_conv_out, w_out, g_post, g_ff)


def _ffn_kernel(u_ref, w1_ref, w2_ref, h_ref, g_ref, o_ref, acc_ref):
    j = pl.program_id(1)

    @pl.when(j == 0)
    def _():
        acc_ref[...] = jnp.zeros_like(acc_ref)

    a = jnp.dot(u_ref[...], w1_ref[...], preferred_element_type=_F32)
    a = jnp.square(jnp.maximum(a, 0.0)).astype(_BF16)
    acc_ref[...] += jnp.dot(a, w2_ref[...], preferred_element_type=_F32)

    @pl.when(j == pl.num_programs(1) - 1)
    def _():
        o_ref[...] = h_ref[...] + _rms(acc_ref[...], g_ref[...], NORM_EPS)


def _ffn(u2, w1, w2, h1, gain, *, tm, tf):
    T, D = h1.shape
    F = w1.shape[1]
    assert T % tm == 0 and F % tf == 0
    return pl.pallas_call(
        _ffn_kernel,
        grid=(T // tm, F // tf),
        in_specs=[
            pl.BlockSpec((tm, D), lambda i, j: (i, 0)),
            pl.BlockSpec((D, tf), lambda i, j: (0, j)),
            pl.BlockSpec((tf, D), lambda i, j: (j, 0)),
            pl.BlockSpec((tm, D), lambda i, j: (i, 0)),
            pl.BlockSpec((1, D), lambda i, j: (0, 0)),
        ],
        out_specs=pl.BlockSpec((tm, D), lambda i, j: (i, 0)),
        out_shape=jax.ShapeDtypeStruct((T, D), _F32),
        scratch_shapes=[pltpu.VMEM((tm, D), _F32)],
        compiler_params=_params("parallel", "arbitrary"),
        name="ffn",
    )(u2, w1, w2, h1, gain)


def _rope_tables(seq):
    d = HEAD_DIM
    inv_freq = 1.0 / (ROPE_THETA ** (jnp.arange(0, d, 2, dtype=_F32) / d))
    ang = jnp.arange(seq, dtype=_F32)[:, None] * inv_freq[None, :]
    cos = jnp.cos(ang)
    sin = jnp.sin(ang)
    return jnp.concatenate([cos, cos], axis=-1), jnp.concatenate([-sin, sin], axis=-1)


def _layer(h2d, l, p, *, batch, seq):
    D = h2d.shape[1]
    lam_init = 0.8 - 0.6 * math.exp(-0.3 * l)
    row = lambda a: a[l][None, :].astype(_F32)
    cos, sin = _rope_tables(seq)
    qk_w = N_HEADS * 2 * HEAD_DIM
    conv_col = 2 * qk_w + N_HEADS * V_HEAD_DIM
    conv_w = p['glu_bias'].shape[1]

    tm_in = min(1024, seq)
    later = [p[name][l].astype(_F32) for name in ('w_conv_out', 'w_out', 'w_ff1', 'w_ff2')]
    z, (w_conv_out, w_out, w_ff1, w_ff2) = _inproj(
        h2d, row(p['pre_mix_gain']), p['w_in'][l].astype(_BF16), cos, sin, row(p['glu_bias']),
        later, seq=seq, tm=tm_in, tn=1024)
    attn = _attention(z, row(p['lambda_q1']), row(p['lambda_k1']), row(p['lambda_q2']),
                      row(p['lambda_k2']), row(p['subln_gain']),
                      batch=batch, seq=seq, tq=min(512, seq), lam_init=lam_init)
    h1, u2 = _conv_merge(z, attn, h2d, p['dw_kernel'][l].astype(_F32), row(p['dw_bias']),
                         row(p['conv_ln_gain']), row(p['conv_ln_bias']),
                         w_conv_out, row(p['b_conv_out']),
                         w_out, row(p['post_mix_gain']),
                         row(p['pre_ff_gain']), seq=seq, tc=min(256, seq), val_col=conv_col,
                         ga_col=conv_col + conv_w)
    return _ffn(u2, w_ff1, w_ff2, h1,
                row(p['post_ff_gain']), tm=min(512, seq), tf=1024)


def kernel(x, pre_mix_gain, w_in, lambda_q1, lambda_k1, lambda_q2, lambda_k2, subln_gain, glu_bias,
           dw_kernel, dw_bias, conv_ln_gain, conv_ln_bias, w_conv_out, b_conv_out, w_out,
           post_mix_gain, pre_ff_gain, w_ff1, w_ff2, post_ff_gain):
    B, S, D = x.shape
    p = dict(pre_mix_gain=pre_mix_gain, w_in=w_in, lambda_q1=lambda_q1, lambda_k1=lambda_k1,
             lambda_q2=lambda_q2, lambda_k2=lambda_k2, subln_gain=subln_gain, glu_bias=glu_bias,
             dw_kernel=dw_kernel, dw_bias=dw_bias, conv_ln_gain=conv_ln_gain,
             conv_ln_bias=conv_ln_bias, w_conv_out=w_conv_out, b_conv_out=b_conv_out, w_out=w_out,
             post_mix_gain=post_mix_gain, pre_ff_gain=pre_ff_gain, w_ff1=w_ff1, w_ff2=w_ff2,
             post_ff_gain=post_ff_gain)
    h = x.reshape(B * S, D).astype(_F32)
    for l in range(w_in.shape[0]):
        h = _layer(h, l, p, batch=B, seq=S)
    return h.reshape(B, S, D).astype(x.dtype)
```

```python
import functools
import math

import jax
import jax.numpy as jnp
from jax import lax
from jax.experimental import pallas as pl
from jax.experimental.pallas import tpu as pltpu

N_HEADS = 8
HEAD_DIM = 128
V_HEAD_DIM = 2 * HEAD_DIM
ROPE_THETA = 10000.0
SUBLN_EPS = 1e-5
CONV_WIDTH = 31
LN_EPS = 1e-5
NORM_EPS = 1e-6
NEG_INF = -1e30
LOG2_E = 1.4426950408889634

LANES = 128
SUBLANES = 8
BF16_ROWS = 16
VMEM_LIMIT_BYTES = 56 * 1024 * 1024
HALO = 32

_BF16 = jnp.bfloat16
_F32 = jnp.float32


def _params(*semantics):
    return pltpu.CompilerParams(dimension_semantics=semantics, vmem_limit_bytes=VMEM_LIMIT_BYTES)


def _rms(v, gain, eps):
    return v * lax.rsqrt(jnp.mean(v * v, axis=-1, keepdims=True) + eps) * gain


def _inproj_kernel(x_ref, g_ref, w_ref, cos_ref, sin_ref, b_ref, *rest,
                   tn, qk_tiles, v_tiles, conv_tiles, n_side):
    side_in = rest[:n_side]
    z_ref = rest[n_side]
    side_out = rest[n_side + 1:2 * n_side + 1]
    u_ref = rest[2 * n_side + 1]
    j = pl.program_id(1)

    @pl.when(j == 0)
    def _():
        u_ref[...] = _rms(x_ref[...], g_ref[...], NORM_EPS).astype(_BF16)

    def matmul():
        for src, dst in zip(side_in, side_out):
            dst[...] = src[...].astype(_BF16)
        return jnp.dot(u_ref[...], w_ref[...], preferred_element_type=_F32)

    def rope_store(scale):
        acc = matmul()
        cos = cos_ref[...]
        sin = sin_ref[...]
        for c in range(tn // HEAD_DIM):
            sl = slice(c * HEAD_DIM, (c + 1) * HEAD_DIM)
            a = acc[:, sl]
            r = a * cos + pltpu.roll(a, HEAD_DIM // 2, axis=1) * sin
            if scale is not None:
                r = r * scale
            z_ref[:, sl] = r.astype(_BF16)

    q_end = qk_tiles
    k_end = 2 * qk_tiles
    v_end = k_end + v_tiles
    c_end = v_end + conv_tiles

    @pl.when(j < q_end)
    def _():
        rope_store(HEAD_DIM ** -0.5 * LOG2_E)

    @pl.when((j >= q_end) & (j < k_end))
    def _():
        rope_store(None)

    @pl.when((j >= k_end) & (j < v_end))
    def _():
        z_ref[...] = matmul().astype(_BF16)

    @pl.when((j >= v_end) & (j < c_end))
    def _():
        z_ref[...] = (matmul() + b_ref[...]).astype(_BF16)

    @pl.when(j >= c_end)
    def _():
        z_ref[...] = (0.5 * jnp.tanh(0.5 * matmul()) + 0.5).astype(_BF16)


def _inproj(x2, gain, w_in, cos, sin, glu_bias, later_weights, *, seq, tm, tn):
    T, D = x2.shape
    W = w_in.shape[1]
    qk_w = N_HEADS * 2 * HEAD_DIM
    v_w = N_HEADS * V_HEAD_DIM
    conv_w = glu_bias.shape[1]
    assert qk_w % tn == 0 and v_w % tn == 0 and conv_w % tn == 0 and W % tn == 0
    assert seq % tm == 0 and T % tm == 0
    qk_tiles, v_tiles, conv_tiles = qk_w // tn, v_w // tn, conv_w // tn
    conv0 = 2 * qk_tiles + v_tiles
    pos_blocks = seq // tm
    ni, nj = T // tm, W // tn
    nchunks = 1 << ((ni * nj).bit_length() - 1)
    for w in later_weights:
        nchunks = min(nchunks, w.shape[0] // BF16_ROWS)
    chunk_map = lambda i, j: (jnp.minimum(i * nj + j, nchunks - 1), 0)
    side_specs = []
    for w in later_weights:
        assert w.shape[0] % (nchunks * BF16_ROWS) == 0
        side_specs.append(pl.BlockSpec((w.shape[0] // nchunks, w.shape[1]), chunk_map))
    kern = functools.partial(_inproj_kernel, tn=tn, qk_tiles=qk_tiles, v_tiles=v_tiles,
                             conv_tiles=conv_tiles, n_side=len(later_weights))
    outs = pl.pallas_call(
        kern,
        grid=(ni, nj),
        in_specs=[
            pl.BlockSpec((tm, D), lambda i, j: (i, 0), pipeline_mode=pl.Buffered(1)),
            pl.BlockSpec((1, D), lambda i, j: (0, 0)),
            pl.BlockSpec((D, tn), lambda i, j: (0, j)),
            pl.BlockSpec((tm, HEAD_DIM), lambda i, j: (i % pos_blocks, 0)),
            pl.BlockSpec((tm, HEAD_DIM), lambda i, j: (i % pos_blocks, 0)),
            pl.BlockSpec((1, tn), lambda i, j: (0, jnp.clip(j - conv0, 0, conv_tiles - 1))),
        ] + side_specs,
        out_specs=[pl.BlockSpec((tm, tn), lambda i, j: (i, j))] + side_specs,
        out_shape=[jax.ShapeDtypeStruct((T, W), _BF16)]
        + [jax.ShapeDtypeStruct(w.shape, _BF16) for w in later_weights],
        scratch_shapes=[pltpu.VMEM((tm, D), _BF16)],
        compiler_params=_params("arbitrary", "arbitrary"),
        name="inproj",
    )(x2, gain, w_in, cos, sin, glu_bias, *later_weights)
    return outs[0], outs[1:]


def _attn_kernel(lq1_ref, lk1_ref, lq2_ref, lk2_ref, g_ref, q_ref, k_ref, v_ref, o_ref,
                 s_ref, a_ref, part_ref, stat_ref, *, tq, lam_init):
    S = q_ref.shape[0]
    d = HEAD_DIM
    rc = BF16_ROWS
    nq = S // tq
    lam = (jnp.exp(jnp.sum(lq1_ref[...] * lk1_ref[...], axis=-1, keepdims=True))
           - jnp.exp(jnp.sum(lq2_ref[...] * lk2_ref[...], axis=-1, keepdims=True))
           + lam_init)
    row = lax.broadcasted_iota(jnp.int32, (tq, tq), 0)
    col = lax.broadcasted_iota(jnp.int32, (tq, tq), 1)
    causal = col <= row
    contract_last = (((1,), (1,)), ((), ()))

    def scores(i):
        off = i * tq
        end = off + tq
        sl = s_ref.at[i % 2]
        q = q_ref[off:end, :]
        for h in range(2):
            qh = q[:, h * d:(h + 1) * d]
            if off:
                sl[h, :, :off] = lax.dot_general(qh, k_ref[:off, h * d:(h + 1) * d], contract_last,
                                                 preferred_element_type=_F32)
            sd = lax.dot_general(qh, k_ref[off:end, h * d:(h + 1) * d], contract_last,
                                 preferred_element_type=_F32)
            sl[h, :, off:end] = jnp.where(causal, sd, NEG_INF)

    def output(i):
        off = i * tq
        end = off + tq
        half = tq // 2
        top = jnp.dot(a_ref[:half, :end - half], v_ref[:end - half, :], preferred_element_type=_F32)
        bot = jnp.dot(a_ref[half:, :end], v_ref[:end, :], preferred_element_type=_F32)
        for r0, o in ((off, top), (off + half, bot)):
            o = _rms(o, g_ref[...], SUBLN_EPS) * (1.0 - lam_init)
            o_ref[r0:r0 + half, :] = o.astype(_BF16)

    scores(0)
    for i in range(nq):
        off = i * tq
        end = off + tq
        ncol = end // LANES
        sl = s_ref.at[i % 2]

        def lane_block(ref, h, r0, c):
            return ref[h, pl.ds(r0, rc), c * LANES:(c + 1) * LANES]

        def segments(make_body):
            for g in range(tq // LANES):
                nvalid = off // LANES + g + 1
                body = make_body(nvalid)
                for r in range(g * (LANES // rc), (g + 1) * (LANES // rc)):
                    body(r, 0)

        def max_pass(nvalid):
            def body(r, _):
                r0 = r * rc
                for h in range(2):
                    m = [lane_block(sl, h, r0, c) for c in range(min(2, nvalid))]
                    for c in range(2, nvalid):
                        m[c % 2] = jnp.maximum(m[c % 2], lane_block(sl, h, r0, c))
                    part_ref[h, pl.ds(r0, rc), :] = m[0] if nvalid == 1 else jnp.maximum(m[0], m[1])
                return 0
            return body

        segments(max_pass)
        for h in range(2):
            m = jnp.max(part_ref[h], axis=-1, keepdims=True)
            stat_ref[h] = jnp.broadcast_to(m, (tq, LANES))
        if i + 1 < nq:
            scores(i + 1)

        def exp_pass(nvalid):
            def body(r, _):
                r0 = r * rc
                for h in range(2):
                    m = stat_ref[h, pl.ds(r0, rc), :]
                    acc = None
                    for c in range(nvalid):
                        p = jnp.exp2(lane_block(sl, h, r0, c) - m)
                        sl[h, pl.ds(r0, rc), c * LANES:(c + 1) * LANES] = p
                        acc = p if acc is None else acc + p
                    part_ref[h, pl.ds(r0, rc), :] = acc
                return 0
            return body

        segments(exp_pass)
        l1 = jnp.sum(part_ref[0], axis=-1, keepdims=True)
        l2 = jnp.sum(part_ref[1], axis=-1, keepdims=True)
        stat_ref[0] = jnp.broadcast_to(1.0 / l1, (tq, LANES))
        stat_ref[1] = jnp.broadcast_to(lam / l2, (tq, LANES))
        if i > 0:
            output(i - 1)

        def combine_pass(nvalid):
            def body(r, _):
                r0 = r * rc
                c1 = stat_ref[0, pl.ds(r0, rc), :]
                c2 = stat_ref[1, pl.ds(r0, rc), :]
                for c in range(nvalid):
                    a = lane_block(sl, 0, r0, c) * c1 - lane_block(sl, 1, r0, c) * c2
                    a_ref[pl.ds(r0, rc), c * LANES:(c + 1) * LANES] = a.astype(_BF16)
                if nvalid < ncol:
                    a_ref[pl.ds(r0, rc), nvalid * LANES:end] = jnp.zeros((rc, end - nvalid * LANES), _BF16)
                return 0
            return body

        segments(combine_pass)
    output(nq - 1)


def _attention(z, lq1, lk1, lq2, lk2, subln_gain, *, batch, seq, tq, lam_init):
    T = z.shape[0]
    hw = V_HEAD_DIM
    assert seq % tq == 0 and tq % LANES == 0
    kern = functools.partial(_attn_kernel, tq=tq, lam_init=lam_init)
    vec = pl.BlockSpec((1, HEAD_DIM), lambda b, h: (0, 0))
    return pl.pallas_call(
        kern,
        grid=(batch, N_HEADS),
        in_specs=[
            vec, vec, vec, vec,
            pl.BlockSpec((1, hw), lambda b, h: (0, 0)),
            pl.BlockSpec((seq, hw), lambda b, h: (b, h)),
            pl.BlockSpec((seq, hw), lambda b, h: (b, N_HEADS + h)),
            pl.BlockSpec((seq, hw), lambda b, h: (b, 2 * N_HEADS + h)),
        ],
        out_specs=pl.BlockSpec((seq, hw), lambda b, h: (b, h)),
        out_shape=jax.ShapeDtypeStruct((T, N_HEADS * hw), _BF16),
        scratch_shapes=[pltpu.VMEM((2, 2, tq, seq), _F32), pltpu.VMEM((tq, seq), _BF16),
                        pltpu.VMEM((2, tq, LANES), _F32), pltpu.VMEM((2, tq, LANES), _F32)],
        compiler_params=_params("parallel", "parallel"),
        name="diff_attn",
    )(lq1, lk1, lq2, lk2, subln_gain, z, z, z)


def _conv_merge_kernel(zv_ref, zg_ref, hv_ref, hg_ref, w_ref, b_ref, lg_ref, lb_ref,
                       a_ref, ga_ref, gc_ref, x_ref, wc_ref, bc_ref, wo_ref, gpost_ref, gff_ref,
                       h_ref, u_ref, g_scr, y_scr, c_scr, m_scr, yo_scr, *, rc, pw, sblocks, nblk):
    t = pl.program_id(0)
    tc, C = c_scr.shape
    D = yo_scr.shape[1]

    @pl.when(t == 0)
    def _():
        c_scr[...] = jnp.zeros_like(c_scr)

    nchunk = D // pw

    def merge_chunk(n):
        cols = slice(n * pw, (n + 1) * pw)
        conv_out = jnp.dot(c_scr[...], wc_ref[:, cols], preferred_element_type=_F32) + bc_ref[:, cols]
        m = (ga_ref[:, cols].astype(_F32) * a_ref[:, cols].astype(_F32)
             + gc_ref[:, cols].astype(_F32) * conv_out)
        m_scr[:, cols] = m.astype(_BF16)

    def out_chunk(n):
        cols = slice(n * pw, (n + 1) * pw)
        yo_scr[:, cols] = jnp.dot(m_scr[...], wo_ref[:, cols], preferred_element_type=_F32)

    pieces = ([(merge_chunk, n, c_scr) for n in range(nchunk)]
              + [(out_chunk, n, m_scr) for n in range(nchunk)])

    def pace(operand_ref, produced):
        bits = pltpu.bitcast(produced[:BF16_ROWS, :], jnp.uint32)
        zero = pltpu.bitcast(lax.shift_right_logical(lax.shift_right_logical(bits, jnp.uint32(31)),
                                                     jnp.uint32(1)), _F32)
        tile = operand_ref[:BF16_ROWS, :LANES]
        operand_ref[:BF16_ROWS, :LANES] = tile + zero.astype(tile.dtype)

    def glu(val, gate):
        return val.astype(_F32) * jax.nn.sigmoid(gate.astype(_F32))

    seq_start = jnp.minimum(t, nblk - 1) % sblocks == 0
    g_scr[:HALO, :] = jnp.where(seq_start, 0.0, glu(hv_ref[...], hg_ref[...]))
    g_scr[HALO:, :] = glu(zv_ref[...], zg_ref[...])

    base = HALO - (CONV_WIDTH - 1)
    nchan = C // LANES
    stride = nchan // len(pieces)
    assert stride >= 1
    for ci in range(nchan):
        c0 = ci * LANES
        for r0 in range(0, tc, rc):
            win = g_scr[r0:r0 + rc + HALO, c0:c0 + LANES]
            acc = None
            for r in range(SUBLANES):
                part = None
                for k in range(CONV_WIDTH):
                    if (base + k) % SUBLANES == r:
                        term = w_ref[k:k + 1, c0:c0 + LANES] * win[base + k:base + k + rc, :]
                        part = term if part is None else part + term
                acc = part if acc is None else acc + part
            y_scr[r0:r0 + rc, c0:c0 + LANES] = acc
            if ci % stride == 0:
                for s0 in range(0, rc, BF16_ROWS):
                    done = acc[s0:s0 + BF16_ROWS, :] if r0 == 0 and s0 == 0 else done + acc[s0:s0 + BF16_ROWS, :]
        if ci % stride == 0 and ci // stride < len(pieces):
            fn, n, operand_ref = pieces[ci // stride]
            pace(operand_ref, done)
            fn(n)

    h = x_ref[...] + _rms(yo_scr[...], gpost_ref[...], NORM_EPS)
    h_ref[...] = h
    u_ref[...] = _rms(h, gff_ref[...], NORM_EPS).astype(_BF16)

    yb = y_scr[...] + b_ref[...]
    mu = jnp.mean(yb, axis=-1, keepdims=True)
    yc = yb - mu
    yn = yc * lax.rsqrt(jnp.mean(yc * yc, axis=-1, keepdims=True) + LN_EPS)
    yn = yn * lg_ref[...] + lb_ref[...]
    c_scr[...] = (yn * jax.nn.sigmoid(yn)).astype(_BF16)


def _conv_merge(z, attn, x2, dw_kernel, dw_bias, ln_gain, ln_bias, w_conv_out, b_conv_out, w_out,
                g_post, g_ff, *, seq, tc, val_col, ga_col, rc=64, pw=512):
    T, D = x2.shape
    C = dw_bias.shape[1]
    assert seq % tc == 0 and T % tc == 0 and tc % HALO == 0 and tc % rc == 0
    assert val_col % C == 0 and ga_col % D == 0 and C == D and D % pw == 0
    vb = val_col // C
    gb = ga_col // D
    sblocks = seq // tc
    nblk = T // tc
    hpb = tc // HALO

    conv_blk = lambda t: jnp.minimum(t, nblk - 1)
    proj_blk = lambda t: jnp.maximum(t - 1, 0)
    conv_row = lambda col: pl.BlockSpec((tc, C), lambda t: (conv_blk(t), col))
    halo = lambda col: pl.BlockSpec((HALO, C), lambda t: (jnp.maximum(conv_blk(t) * hpb - 1, 0), col))
    proj_row = lambda col: pl.BlockSpec((tc, D), lambda t: (proj_blk(t), col))
    const = lambda shape: pl.BlockSpec(shape, lambda t: (0, 0), pipeline_mode=pl.Buffered(1))
    kern = functools.partial(_conv_merge_kernel, rc=rc, pw=pw, sblocks=sblocks, nblk=nblk)
    return pl.pallas_call(
        kern,
        grid=(nblk + 1,),
        in_specs=[conv_row(vb), conv_row(vb + 1), halo(vb), halo(vb + 1),
                  const((CONV_WIDTH, C)), const((1, C)), const((1, C)), const((1, C)),
                  proj_row(0), proj_row(gb), proj_row(gb + 1), proj_row(0),
                  const((C, D)), const((1, D)), const((D, D)), const((1, D)), const((1, D))],
        out_specs=[proj_row(0), proj_row(0)],
        out_shape=[jax.ShapeDtypeStruct((T, D), _F32), jax.ShapeDtypeStruct((T, D), _BF16)],
        scratch_shapes=[pltpu.VMEM((tc + HALO, C), _F32), pltpu.VMEM((tc, C), _F32),
                        pltpu.VMEM((tc, C), _BF16), pltpu.VMEM((tc, D), _BF16),
                        pltpu.VMEM((tc, D), _F32)],
        compiler_params=_params("arbitrary"),
        name="conv_merge",
    )(z, z, z, z, dw_kernel, dw_bias, ln_gain, ln_bias, attn, z, z, x2,
      w_conv_out, b_conv_out, w_out, g_post, g_ff)


def _ffn_kernel(u_ref, w1_ref, w2_ref, h_ref, g_ref, o_ref):
    j = pl.program_id(1)

    @pl.when(j == 0)
    def _():
        o_ref[...] = jnp.zeros_like(o_ref)

    a = jnp.dot(u_ref[...], w1_ref[...], preferred_element_type=_F32)
    a = jnp.square(jnp.maximum(a, 0.0)).astype(_BF16)
    o_ref[...] += jnp.dot(a, w2_ref[...], preferred_element_type=_F32)

    @pl.when(j == pl.num_programs(1) - 1)
    def _():
        o_ref[...] = h_ref[...] + _rms(o_ref[...], g_ref[...], NORM_EPS)


def _ffn(u2, w1, w2, h1, gain, *, tm, tf):
    T, D = h1.shape
    F = w1.shape[1]
    assert T % tm == 0 and F % tf == 0
    return pl.pallas_call(
        _ffn_kernel,
        grid=(T // tm, F // tf),
        in_specs=[
            pl.BlockSpec((tm, D), lambda i, j: (i, 0)),
            pl.BlockSpec((D, tf), lambda i, j: (0, j)),
            pl.BlockSpec((tf, D), lambda i, j: (j, 0)),
            pl.BlockSpec((tm, D), lambda i, j: (i, 0), pipeline_mode=pl.Buffered(1)),
            pl.BlockSpec((1, D), lambda i, j: (0, 0)),
        ],
        out_specs=pl.BlockSpec((tm, D), lambda i, j: (i, 0)),
        out_shape=jax.ShapeDtypeStruct((T, D), _F32),
        compiler_params=_params("parallel", "arbitrary"),
        name="ffn",
    )(u2, w1, w2, h1, gain)


def _rope_tables(seq):
    d = HEAD_DIM
    inv_freq = 1.0 / (ROPE_THETA ** (jnp.arange(0, d, 2, dtype=_F32) / d))
    ang = jnp.arange(seq, dtype=_F32)[:, None] * inv_freq[None, :]
    cos = jnp.cos(ang)
    sin = jnp.sin(ang)
    return jnp.concatenate([cos, cos], axis=-1), jnp.concatenate([-sin, sin], axis=-1)


def _layer(h2d, l, p, *, batch, seq):
    D = h2d.shape[1]
    lam_init = 0.8 - 0.6 * math.exp(-0.3 * l)
    row = lambda a: a[l][None, :].astype(_F32)
    cos, sin = _rope_tables(seq)
    qk_w = N_HEADS * 2 * HEAD_DIM
    conv_col = 2 * qk_w + N_HEADS * V_HEAD_DIM
    conv_w = p['glu_bias'].shape[1]

    tm_in = min(1024, seq)
    later = [p[name][l].astype(_F32) for name in ('w_conv_out', 'w_out', 'w_ff1', 'w_ff2')]
    z, (w_conv_out, w_out, w_ff1, w_ff2) = _inproj(
        h2d, row(p['pre_mix_gain']), p['w_in'][l].astype(_BF16), cos, sin, row(p['glu_bias']),
        later, seq=seq, tm=tm_in, tn=2048)
    attn = _attention(z, row(p['lambda_q1']), row(p['lambda_k1']), row(p['lambda_q2']),
                      row(p['lambda_k2']), row(p['subln_gain']),
                      batch=batch, seq=seq, tq=min(512, seq), lam_init=lam_init)
    h1, u2 = _conv_merge(z, attn, h2d, p['dw_kernel'][l].astype(_F32), row(p['dw_bias']),
                         row(p['conv_ln_gain']), row(p['conv_ln_bias']),
                         w_conv_out, row(p['b_conv_out']),
                         w_out, row(p['post_mix_gain']),
                         row(p['pre_ff_gain']), seq=seq, tc=min(256, seq), val_col=conv_col,
                         ga_col=conv_col + conv_w)
    return _ffn(u2, w_ff1, w_ff2, h1,
                row(p['post_ff_gain']), tm=min(1024, seq), tf=512)


def kernel(x, pre_mix_gain, w_in, lambda_q1, lambda_k1, lambda_q2, lambda_k2, subln_gain, glu_bias,
           dw_kernel, dw_bias, conv_ln_gain, conv_ln_bias, w_conv_out, b_conv_out, w_out,
           post_mix_gain, pre_ff_gain, w_ff1, w_ff2, post_ff_gain):
    B, S, D = x.shape
    p = dict(pre_mix_gain=pre_mix_gain, w_in=w_in, lambda_q1=lambda_q1, lambda_k1=lambda_k1,
             lambda_q2=lambda_q2, lambda_k2=lambda_k2, subln_gain=subln_gain, glu_bias=glu_bias,
             dw_kernel=dw_kernel, dw_bias=dw_bias, conv_ln_gain=conv_ln_gain,
             conv_ln_bias=conv_ln_bias, w_conv_out=w_conv_out, b_conv_out=b_conv_out, w_out=w_out,
             post_mix_gain=post_mix_gain, pre_ff_gain=pre_ff_gain, w_ff1=w_ff1, w_ff2=w_ff2,
             post_ff_gain=post_ff_gain)
    h = x.reshape(B * S, D).astype(_F32)
    for l in range(w_in.shape[0]):
        h = _layer(h, l, p, batch=B, seq=S)
    return h.reshape(B, S, D).astype(x.dtype)
```

```python
import functools
import math

import jax
import jax.numpy as jnp
from jax import lax
from jax.experimental import pallas as pl
from jax.experimental.pallas import tpu as pltpu

N_HEADS = 8
HEAD_DIM = 128
V_HEAD_DIM = 2 * HEAD_DIM
ROPE_THETA = 10000.0
SUBLN_EPS = 1e-5
CONV_WIDTH = 31
LN_EPS = 1e-5
NORM_EPS = 1e-6
NEG_INF = -1e30
LOG2_E = 1.4426950408889634

LANES = 128
SUBLANES = 8
BF16_ROWS = 16
MXU_COLS = 256
GLU_PAIR = 2 * MXU_COLS
VMEM_LIMIT_BYTES = 56 * 1024 * 1024
HALO = 32

_BF16 = jnp.bfloat16
_F32 = jnp.float32


def _params(*semantics):
    return pltpu.CompilerParams(dimension_semantics=semantics, vmem_limit_bytes=VMEM_LIMIT_BYTES)


def _rms(v, gain, eps):
    return v * lax.rsqrt(jnp.mean(v * v, axis=-1, keepdims=True) + eps) * gain


def _inproj_kernel(x_ref, g_ref, w_ref, cos_ref, sin_ref, b_ref, *rest,
                   tn, qk_tiles, v_tiles, conv_tiles, n_side):
    side_in = rest[:n_side]
    z_ref, glu_ref = rest[n_side:n_side + 2]
    side_out = rest[n_side + 2:2 * n_side + 2]
    u_ref = rest[2 * n_side + 2]
    j = pl.program_id(1)

    @pl.when(j == 0)
    def _():
        u_ref[...] = _rms(x_ref[...], g_ref[...], NORM_EPS).astype(_BF16)

    def matmul():
        for src, dst in zip(side_in, side_out):
            dst[...] = src[...].astype(_BF16)
        return jnp.dot(u_ref[...], w_ref[...], preferred_element_type=_F32)

    def rope_store(scale):
        acc = matmul()
        cos = cos_ref[...]
        sin = sin_ref[...]
        for c in range(tn // HEAD_DIM):
            sl = slice(c * HEAD_DIM, (c + 1) * HEAD_DIM)
            a = acc[:, sl]
            r = a * cos + pltpu.roll(a, HEAD_DIM // 2, axis=1) * sin
            if scale is not None:
                r = r * scale
            z_ref[:, sl] = r.astype(_BF16)

    def sigmoid(a):
        return 0.5 * jnp.tanh(0.5 * a) + 0.5

    q_end = qk_tiles
    k_end = 2 * qk_tiles
    v_end = k_end + v_tiles
    c_end = v_end + conv_tiles

    @pl.when(j < q_end)
    def _():
        rope_store(HEAD_DIM ** -0.5 * LOG2_E)

    @pl.when((j >= q_end) & (j < k_end))
    def _():
        rope_store(None)

    @pl.when((j >= k_end) & (j < v_end))
    def _():
        z_ref[...] = matmul().astype(_BF16)

    @pl.when((j >= v_end) & (j < c_end))
    def _():
        acc = matmul() + b_ref[...]
        for c in range(tn // GLU_PAIR):
            val = acc[:, c * GLU_PAIR:c * GLU_PAIR + GLU_PAIR // 2]
            gate = acc[:, c * GLU_PAIR + GLU_PAIR // 2:(c + 1) * GLU_PAIR]
            glu_ref[:, c * GLU_PAIR // 2:(c + 1) * GLU_PAIR // 2] = (val * sigmoid(gate)).astype(_BF16)

    @pl.when(j >= c_end)
    def _():
        z_ref[...] = sigmoid(matmul()).astype(_BF16)


def _pair_glu_columns(a):
    rows, width = a.shape
    half = GLU_PAIR // 2
    return a.reshape(rows, 2, width // GLU_PAIR, half).transpose(0, 2, 1, 3).reshape(rows, width)


def _inproj(x2, gain, w_in, cos, sin, glu_bias, later_weights, *, seq, tm, tn):
    T, D = x2.shape
    W = w_in.shape[1]
    qk_w = N_HEADS * 2 * HEAD_DIM
    v_w = N_HEADS * V_HEAD_DIM
    conv_w = glu_bias.shape[1]
    assert qk_w % tn == 0 and v_w % tn == 0 and conv_w % tn == 0 and W % tn == 0
    assert tn % GLU_PAIR == 0
    assert seq % tm == 0 and T % tm == 0
    qk_tiles, v_tiles, conv_tiles = qk_w // tn, v_w // tn, conv_w // tn
    conv0 = 2 * qk_tiles + v_tiles
    pos_blocks = seq // tm
    ni, nj = T // tm, W // tn
    z_tile = lambda j: jnp.where(j < conv0, j, jnp.maximum(j - conv_tiles, conv0))
    glu_tile = lambda j: jnp.clip(j - conv0, 0, conv_tiles - 1)
    nchunks = 1 << ((ni * nj).bit_length() - 1)
    for w in later_weights:
        nchunks = min(nchunks, w.shape[0] // BF16_ROWS)
    chunk_map = lambda i, j: (jnp.minimum(i * nj + j, nchunks - 1), 0)
    side_specs = []
    for w in later_weights:
        assert w.shape[0] % (nchunks * BF16_ROWS) == 0
        side_specs.append(pl.BlockSpec((w.shape[0] // nchunks, w.shape[1]), chunk_map))
    kern = functools.partial(_inproj_kernel, tn=tn, qk_tiles=qk_tiles, v_tiles=v_tiles,
                             conv_tiles=conv_tiles, n_side=len(later_weights))
    outs = pl.pallas_call(
        kern,
        grid=(ni, nj),
        in_specs=[
            pl.BlockSpec((tm, D), lambda i, j: (i, 0)),
            pl.BlockSpec((1, D), lambda i, j: (0, 0)),
            pl.BlockSpec((D, tn), lambda i, j: (0, j)),
            pl.BlockSpec((tm, HEAD_DIM), lambda i, j: (i % pos_blocks, 0)),
            pl.BlockSpec((tm, HEAD_DIM), lambda i, j: (i % pos_blocks, 0)),
            pl.BlockSpec((1, tn), lambda i, j: (0, glu_tile(j))),
        ] + side_specs,
        out_specs=[pl.BlockSpec((tm, tn), lambda i, j: (i, z_tile(j))),
                   pl.BlockSpec((tm, tn // 2), lambda i, j: (i, glu_tile(j)))] + side_specs,
        out_shape=[jax.ShapeDtypeStruct((T, W - conv_w), _BF16),
                   jax.ShapeDtypeStruct((T, conv_w // 2), _BF16)]
        + [jax.ShapeDtypeStruct(w.shape, _BF16) for w in later_weights],
        scratch_shapes=[pltpu.VMEM((tm, D), _BF16)],
        compiler_params=_params("arbitrary", "arbitrary"),
        name="inproj",
    )(x2, gain, w_in, cos, sin, glu_bias, *later_weights)
    return outs[0], outs[1], outs[2:]


def _attn_kernel(lq1_ref, lk1_ref, lq2_ref, lk2_ref, g_ref, q_ref, k_ref, v_ref, o_ref,
                 s_ref, a_ref, part_ref, stat_ref, *, tq, lam_init):
    S = q_ref.shape[0]
    d = HEAD_DIM
    rc = BF16_ROWS
    nq = S // tq
    lam = (jnp.exp(jnp.sum(lq1_ref[...] * lk1_ref[...], axis=-1, keepdims=True))
           - jnp.exp(jnp.sum(lq2_ref[...] * lk2_ref[...], axis=-1, keepdims=True))
           + lam_init)
    row = lax.broadcasted_iota(jnp.int32, (tq, tq), 0)
    col = lax.broadcasted_iota(jnp.int32, (tq, tq), 1)
    causal = col <= row
    contract_last = (((1,), (1,)), ((), ()))

    def scores(i):
        off = i * tq
        end = off + tq
        sl = s_ref.at[i % 2]
        q = q_ref[off:end, :]
        for h in range(2):
            qh = q[:, h * d:(h + 1) * d]
            if off:
                sl[h, :, :off] = lax.dot_general(qh, k_ref[:off, h * d:(h + 1) * d], contract_last,
                                                 preferred_element_type=_F32)
            sd = lax.dot_general(qh, k_ref[off:end, h * d:(h + 1) * d], contract_last,
                                 preferred_element_type=_F32)
            sl[h, :, off:end] = jnp.where(causal, sd, NEG_INF)

    def output(i):
        off = i * tq
        end = off + tq
        half = tq // 2
        top = jnp.dot(a_ref[:half, :end - half], v_ref[:end - half, :], preferred_element_type=_F32)
        bot = jnp.dot(a_ref[half:, :end], v_ref[:end, :], preferred_element_type=_F32)
        for r0, o in ((off, top), (off + half, bot)):
            o = _rms(o, g_ref[...], SUBLN_EPS) * (1.0 - lam_init)
            o_ref[r0:r0 + half, :] = o.astype(_BF16)

    scores(0)
    for i in range(nq):
        off = i * tq
        end = off + tq
        ncol = end // LANES
        sl = s_ref.at[i % 2]

        def lane_block(ref, h, r0, c):
            return ref[h, pl.ds(r0, rc), c * LANES:(c + 1) * LANES]

        def segments(make_body):
            for g in range(tq // LANES):
                nvalid = off // LANES + g + 1
                body = make_body(nvalid)
                for r in range(g * (LANES // rc), (g + 1) * (LANES // rc)):
                    body(r, 0)

        def max_pass(nvalid):
            def body(r, _):
                r0 = r * rc
                for h in range(2):
                    m = [lane_block(sl, h, r0, c) for c in range(min(2, nvalid))]
                    for c in range(2, nvalid):
                        m[c % 2] = jnp.maximum(m[c % 2], lane_block(sl, h, r0, c))
                    part_ref[h, pl.ds(r0, rc), :] = m[0] if nvalid == 1 else jnp.maximum(m[0], m[1])
                return 0
            return body

        segments(max_pass)
        for h in range(2):
            m = jnp.max(part_ref[h], axis=-1, keepdims=True)
            stat_ref[h] = jnp.broadcast_to(m, (tq, LANES))
        if i + 1 < nq:
            scores(i + 1)

        def exp_pass(nvalid):
            def body(r, _):
                r0 = r * rc
                for h in range(2):
                    m = stat_ref[h, pl.ds(r0, rc), :]
                    acc = None
                    for c in range(nvalid):
                        p = jnp.exp2(lane_block(sl, h, r0, c) - m)
                        sl[h, pl.ds(r0, rc), c * LANES:(c + 1) * LANES] = p
                        acc = p if acc is None else acc + p
                    part_ref[h, pl.ds(r0, rc), :] = acc
                return 0
            return body

        segments(exp_pass)
        l1 = jnp.sum(part_ref[0], axis=-1, keepdims=True)
        l2 = jnp.sum(part_ref[1], axis=-1, keepdims=True)
        stat_ref[0] = jnp.broadcast_to(1.0 / l1, (tq, LANES))
        stat_ref[1] = jnp.broadcast_to(lam / l2, (tq, LANES))
        if i > 0:
            output(i - 1)

        def combine_pass(nvalid):
            def body(r, _):
                r0 = r * rc
                c1 = stat_ref[0, pl.ds(r0, rc), :]
                c2 = stat_ref[1, pl.ds(r0, rc), :]
                for c in range(nvalid):
                    a = lane_block(sl, 0, r0, c) * c1 - lane_block(sl, 1, r0, c) * c2
                    a_ref[pl.ds(r0, rc), c * LANES:(c + 1) * LANES] = a.astype(_BF16)
                if nvalid < ncol:
                    a_ref[pl.ds(r0, rc), nvalid * LANES:end] = jnp.zeros((rc, end - nvalid * LANES), _BF16)
                return 0
            return body

        segments(combine_pass)
    output(nq - 1)


def _attention(z, lq1, lk1, lq2, lk2, subln_gain, *, batch, seq, tq, lam_init):
    T = z.shape[0]
    hw = V_HEAD_DIM
    assert seq % tq == 0 and tq % LANES == 0
    kern = functools.partial(_attn_kernel, tq=tq, lam_init=lam_init)
    vec = pl.BlockSpec((1, HEAD_DIM), lambda b, h: (0, 0))
    return pl.pallas_call(
        kern,
        grid=(batch, N_HEADS),
        in_specs=[
            vec, vec, vec, vec,
            pl.BlockSpec((1, hw), lambda b, h: (0, 0)),
            pl.BlockSpec((seq, hw), lambda b, h: (b, h)),
            pl.BlockSpec((seq, hw), lambda b, h: (b, N_HEADS + h)),
            pl.BlockSpec((seq, hw), lambda b, h: (b, 2 * N_HEADS + h)),
        ],
        out_specs=pl.BlockSpec((seq, hw), lambda b, h: (b, h)),
        out_shape=jax.ShapeDtypeStruct((T, N_HEADS * hw), _BF16),
        scratch_shapes=[pltpu.VMEM((2, 2, tq, seq), _F32), pltpu.VMEM((tq, seq), _BF16),
                        pltpu.VMEM((2, tq, LANES), _F32), pltpu.VMEM((2, tq, LANES), _F32)],
        compiler_params=_params("parallel", "parallel"),
        name="diff_attn",
    )(lq1, lk1, lq2, lk2, subln_gain, z, z, z)


def _conv_merge_kernel(glu_ref, halo_ref, w_ref, b_ref, lg_ref, lb_ref,
                       a_ref, ga_ref, gc_ref, x_ref, wc_ref, bc_ref, wo_ref, gpost_ref, gff_ref,
                       h_ref, u_ref, g_scr, y_scr, c_scr, m_scr, yo_scr, *, rc, pw, sblocks, nblk):
    t = pl.program_id(0)
    tc, C = c_scr.shape
    D = yo_scr.shape[1]

    @pl.when(t == 0)
    def _():
        c_scr[...] = jnp.zeros_like(c_scr)

    nchunk = D // pw

    def merge_chunk(n):
        cols = slice(n * pw, (n + 1) * pw)
        conv_out = jnp.dot(c_scr[...], wc_ref[:, cols], preferred_element_type=_F32) + bc_ref[:, cols]
        m = (ga_ref[:, cols].astype(_F32) * a_ref[:, cols].astype(_F32)
             + gc_ref[:, cols].astype(_F32) * conv_out)
        m_scr[:, cols] = m.astype(_BF16)

    def out_chunk(n):
        cols = slice(n * pw, (n + 1) * pw)
        yo_scr[:, cols] = jnp.dot(m_scr[...], wo_ref[:, cols], preferred_element_type=_F32)

    pieces = ([(merge_chunk, n, c_scr) for n in range(nchunk)]
              + [(out_chunk, n, m_scr) for n in range(nchunk)])

    def pace(operand_ref, produced):
        bits = pltpu.bitcast(produced[:BF16_ROWS, :], jnp.uint32)
        zero = pltpu.bitcast(lax.shift_right_logical(lax.shift_right_logical(bits, jnp.uint32(31)),
                                                     jnp.uint32(1)), _F32)
        tile = operand_ref[:BF16_ROWS, :LANES]
        operand_ref[:BF16_ROWS, :LANES] = tile + zero.astype(tile.dtype)

    seq_start = jnp.minimum(t, nblk - 1) % sblocks == 0
    g_scr[:HALO, :] = jnp.where(seq_start, 0.0, halo_ref[...].astype(_F32))
    g_scr[HALO:, :] = glu_ref[...].astype(_F32)

    base = HALO - (CONV_WIDTH - 1)
    nchan = C // LANES
    stride = nchan // len(pieces)
    assert stride >= 1
    for ci in range(nchan):
        c0 = ci * LANES
        for r0 in range(0, tc, rc):
            win = g_scr[r0:r0 + rc + HALO, c0:c0 + LANES]
            acc = None
            for r in range(SUBLANES):
                part = None
                for k in range(CONV_WIDTH):
                    if (base + k) % SUBLANES == r:
                        term = w_ref[k:k + 1, c0:c0 + LANES] * win[base + k:base + k + rc, :]
                        part = term if part is None else part + term
                acc = part if acc is None else acc + part
            y_scr[r0:r0 + rc, c0:c0 + LANES] = acc
            if ci % stride == 0:
                for s0 in range(0, rc, BF16_ROWS):
                    done = acc[s0:s0 + BF16_ROWS, :] if r0 == 0 and s0 == 0 else done + acc[s0:s0 + BF16_ROWS, :]
        if ci % stride == 0 and ci // stride < len(pieces):
            fn, n, operand_ref = pieces[ci // stride]
            pace(operand_ref, done)
            fn(n)

    h = x_ref[...] + _rms(yo_scr[...], gpost_ref[...], NORM_EPS)
    h_ref[...] = h
    u_ref[...] = _rms(h, gff_ref[...], NORM_EPS).astype(_BF16)

    yb = y_scr[...] + b_ref[...]
    mu = jnp.mean(yb, axis=-1, keepdims=True)
    yc = yb - mu
    yn = yc * lax.rsqrt(jnp.mean(yc * yc, axis=-1, keepdims=True) + LN_EPS)
    yn = yn * lg_ref[...] + lb_ref[...]
    c_scr[...] = (yn * jax.nn.sigmoid(yn)).astype(_BF16)


def _conv_merge(glu, z, attn, x2, dw_kernel, dw_bias, ln_gain, ln_bias, w_conv_out, b_conv_out,
                w_out, g_post, g_ff, *, seq, tc, ga_col, rc=64, pw=512):
    T, D = x2.shape
    C = dw_bias.shape[1]
    assert seq % tc == 0 and T % tc == 0 and tc % HALO == 0 and tc % rc == 0
    assert glu.shape == (T, C) and ga_col % D == 0 and C == D and D % pw == 0
    gb = ga_col // D
    sblocks = seq // tc
    nblk = T // tc
    hpb = tc // HALO

    conv_blk = lambda t: jnp.minimum(t, nblk - 1)
    proj_blk = lambda t: jnp.maximum(t - 1, 0)
    conv_row = pl.BlockSpec((tc, C), lambda t: (conv_blk(t), 0))
    halo = pl.BlockSpec((HALO, C), lambda t: (jnp.maximum(conv_blk(t) * hpb - 1, 0), 0))
    proj_row = lambda col: pl.BlockSpec((tc, D), lambda t: (proj_blk(t), col))
    const = lambda shape: pl.BlockSpec(shape, lambda t: (0, 0), pipeline_mode=pl.Buffered(1))
    kern = functools.partial(_conv_merge_kernel, rc=rc, pw=pw, sblocks=sblocks, nblk=nblk)
    return pl.pallas_call(
        kern,
        grid=(nblk + 1,),
        in_specs=[conv_row, halo,
                  const((CONV_WIDTH, C)), const((1, C)), const((1, C)), const((1, C)),
                  proj_row(0), proj_row(gb), proj_row(gb + 1), proj_row(0),
                  const((C, D)), const((1, D)), const((D, D)), const((1, D)), const((1, D))],
        out_specs=[proj_row(0), proj_row(0)],
        out_shape=[jax.ShapeDtypeStruct((T, D), _F32), jax.ShapeDtypeStruct((T, D), _BF16)],
        scratch_shapes=[pltpu.VMEM((tc + HALO, C), _F32), pltpu.VMEM((tc, C), _F32),
                        pltpu.VMEM((tc, C), _BF16), pltpu.VMEM((tc, D), _BF16),
                        pltpu.VMEM((tc, D), _F32)],
        compiler_params=_params("arbitrary"),
        name="conv_merge",
    )(glu, glu, dw_kernel, dw_bias, ln_gain, ln_bias, attn, z, z, x2,
      w_conv_out, b_conv_out, w_out, g_post, g_ff)


def _ffn_kernel(u_ref, w1_ref, w2_ref, h_ref, g_ref, o_ref, acc_ref):
    j = pl.program_id(1)

    @pl.when(j == 0)
    def _():
        acc_ref[...] = jnp.zeros_like(acc_ref)

    a = jnp.dot(u_ref[...], w1_ref[...], preferred_element_type=_F32)
    a = jnp.square(jnp.maximum(a, 0.0)).astype(_BF16)
    acc_ref[...] += jnp.dot(a, w2_ref[...], preferred_element_type=_F32)

    @pl.when(j == pl.num_programs(1) - 1)
    def _():
        o_ref[...] = h_ref[...] + _rms(acc_ref[...], g_ref[...], NORM_EPS)


def _ffn(u2, w1, w2, h1, gain, *, tm, tf):
    T, D = h1.shape
    F = w1.shape[1]
    assert T % tm == 0 and F % tf == 0
    return pl.pallas_call(
        _ffn_kernel,
        grid=(T // tm, F // tf),
        in_specs=[
            pl.BlockSpec((tm, D), lambda i, j: (i, 0)),
            pl.BlockSpec((D, tf), lambda i, j: (0, j)),
            pl.BlockSpec((tf, D), lambda i, j: (j, 0)),
            pl.BlockSpec((tm, D), lambda i, j: (i, 0)),
            pl.BlockSpec((1, D), lambda i, j: (0, 0)),
        ],
        out_specs=pl.BlockSpec((tm, D), lambda i, j: (i, 0)),
        out_shape=jax.ShapeDtypeStruct((T, D), _F32),
        scratch_shapes=[pltpu.VMEM((tm, D), _F32)],
        compiler_params=_params("parallel", "arbitrary"),
        name="ffn",
    )(u2, w1, w2, h1, gain)


def _rope_tables(seq):
    d = HEAD_DIM
    inv_freq = 1.0 / (ROPE_THETA ** (jnp.arange(0, d, 2, dtype=_F32) / d))
    ang = jnp.arange(seq, dtype=_F32)[:, None] * inv_freq[None, :]
    cos = jnp.cos(ang)
    sin = jnp.sin(ang)
    return jnp.concatenate([cos, cos], axis=-1), jnp.concatenate([-sin, sin], axis=-1)


def _layer(h2d, l, p, *, batch, seq):
    D = h2d.shape[1]
    lam_init = 0.8 - 0.6 * math.exp(-0.3 * l)
    row = lambda a: a[l][None, :].astype(_F32)
    cos, sin = _rope_tables(seq)
    qk_w = N_HEADS * 2 * HEAD_DIM
    conv_col = 2 * qk_w + N_HEADS * V_HEAD_DIM
    conv_w = p['glu_bias'].shape[1]

    tm_in = min(1024, seq)
    tn_in = 1024
    w_in = p['w_in'][l]
    w_in = jnp.concatenate([w_in[:, :conv_col],
                            _pair_glu_columns(w_in[:, conv_col:conv_col + conv_w]),
                            w_in[:, conv_col + conv_w:]], axis=1).astype(_BF16)
    glu_bias = _pair_glu_columns(row(p['glu_bias']))
    later = [p[name][l].astype(_F32) for name in ('w_conv_out', 'w_out', 'w_ff1', 'w_ff2')]
    z, glu, (w_conv_out, w_out, w_ff1, w_ff2) = _inproj(
        h2d, row(p['pre_mix_gain']), w_in, cos, sin, glu_bias, later, seq=seq, tm=tm_in, tn=tn_in)
    attn = _attention(z, row(p['lambda_q1']), row(p['lambda_k1']), row(p['lambda_q2']),
                      row(p['lambda_k2']), row(p['subln_gain']),
                      batch=batch, seq=seq, tq=min(512, seq), lam_init=lam_init)
    h1, u2 = _conv_merge(glu, z, attn, h2d, p['dw_kernel'][l].astype(_F32), row(p['dw_bias']),
                         row(p['conv_ln_gain']), row(p['conv_ln_bias']),
                         w_conv_out, row(p['b_conv_out']),
                         w_out, row(p['post_mix_gain']),
                         row(p['pre_ff_gain']), seq=seq, tc=min(256, seq), ga_col=conv_col)
    return _ffn(u2, w_ff1, w_ff2, h1,
                row(p['post_ff_gain']), tm=min(512, seq), tf=1024)


def kernel(x, pre_mix_gain, w_in, lambda_q1, lambda_k1, lambda_q2, lambda_k2, subln_gain, glu_bias,
           dw_kernel, dw_bias, conv_ln_gain, conv_ln_bias, w_conv_out, b_conv_out, w_out,
           post_mix_gain, pre_ff_gain, w_ff1, w_ff2, post_ff_gain):
    B, S, D = x.shape
    p = dict(pre_mix_gain=pre_mix_gain, w_in=w_in, lambda_q1=lambda_q1, lambda_k1=lambda_k1,
             lambda_q2=lambda_q2, lambda_k2=lambda_k2, subln_gain=subln_gain, glu_bias=glu_bias,
             dw_kernel=dw_kernel, dw_bias=dw_bias, conv_ln_gain=conv_ln_gain,
             conv_ln_bias=conv_ln_bias, w_conv_out=w_conv_out, b_conv_out=b_conv_out, w_out=w_out,
             post_mix_gain=post_mix_gain, pre_ff_gain=pre_ff_gain, w_ff1=w_ff1, w_ff2=w_ff2,
             post_ff_gain=post_ff_gain)
    h = x.reshape(B * S, D).astype(_F32)
    for l in range(w_in.shape[0]):
        h = _layer(h, l, p, batch=B, seq=S)
    return h.reshape(B, S, D).astype(x.dtype)
```

```python
import functools
import math

import jax
import jax.numpy as jnp
from jax import lax
from jax.experimental import pallas as pl
from jax.experimental.pallas import tpu as pltpu

N_HEADS = 8
HEAD_DIM = 128
V_HEAD_DIM = 2 * HEAD_DIM
ROPE_THETA = 10000.0
SUBLN_EPS = 1e-5
CONV_WIDTH = 31
LN_EPS = 1e-5
NORM_EPS = 1e-6
NEG_INF = -1e30
LOG2_E = 1.4426950408889634

LANES = 128
SUBLANES = 8
BF16_ROWS = 16
MXU_COLS = 256
GLU_PAIR = 2 * MXU_COLS
VMEM_LIMIT_BYTES = 56 * 1024 * 1024
HALO = 32

_BF16 = jnp.bfloat16
_F32 = jnp.float32


def _params(*semantics):
    return pltpu.CompilerParams(dimension_semantics=semantics, vmem_limit_bytes=VMEM_LIMIT_BYTES)


def _rms(v, gain, eps):
    return v * lax.rsqrt(jnp.mean(v * v, axis=-1, keepdims=True) + eps) * gain


def _inproj_kernel(x_ref, g_ref, w_ref, cos_ref, sin_ref, bv_ref, bg_ref, *rest,
                   tn, qk_tiles, v_tiles, conv_tiles, n_side):
    side_in = rest[:n_side]
    z_ref, glu_ref = rest[n_side:n_side + 2]
    side_out = rest[n_side + 2:2 * n_side + 2]
    u_ref = rest[2 * n_side + 2]
    j = pl.program_id(1)

    @pl.when(j == 0)
    def _():
        u_ref[...] = _rms(x_ref[...], g_ref[...], NORM_EPS).astype(_BF16)

    def matmul():
        for src, dst in zip(side_in, side_out):
            dst[...] = src[...].astype(_BF16)
        return jnp.dot(u_ref[...], w_ref[...], preferred_element_type=_F32)

    def rope_store(scale):
        acc = matmul()
        cos = cos_ref[...]
        sin = sin_ref[...]
        for c in range(tn // HEAD_DIM):
            sl = slice(c * HEAD_DIM, (c + 1) * HEAD_DIM)
            a = acc[:, sl]
            r = a * cos + pltpu.roll(a, HEAD_DIM // 2, axis=1) * sin
            if scale is not None:
                r = r * scale
            z_ref[:, sl] = r.astype(_BF16)

    def sigmoid(a):
        return 0.5 * jnp.tanh(0.5 * a) + 0.5

    q_end = qk_tiles
    k_end = 2 * qk_tiles
    v_end = k_end + v_tiles
    c_end = v_end + conv_tiles

    @pl.when(j < q_end)
    def _():
        rope_store(HEAD_DIM ** -0.5 * LOG2_E)

    @pl.when((j >= q_end) & (j < k_end))
    def _():
        rope_store(None)

    @pl.when((j >= k_end) & (j < v_end))
    def _():
        z_ref[...] = matmul().astype(_BF16)

    @pl.when((j >= v_end) & (j < c_end))
    def _():
        acc = matmul()
        half = GLU_PAIR // 2
        for c in range(tn // GLU_PAIR):
            out_cols = slice(c * half, (c + 1) * half)
            val = acc[:, c * GLU_PAIR:c * GLU_PAIR + half] + bv_ref[:, out_cols]
            gate = acc[:, c * GLU_PAIR + half:(c + 1) * GLU_PAIR] + bg_ref[:, out_cols]
            glu_ref[:, out_cols] = (val * sigmoid(gate)).astype(_BF16)

    @pl.when(j >= c_end)
    def _():
        z_ref[...] = sigmoid(matmul()).astype(_BF16)


def _cast_kernel(src_ref, dst_ref):
    dst_ref[...] = src_ref[...].astype(dst_ref.dtype)


def _paired_bf16(w_in, conv_col, conv_w):
    D, W = w_in.shape
    half = GLU_PAIR // 2
    assert W % half == 0 and conv_col % half == 0 and conv_w % GLU_PAIR == 0
    first, groups = conv_col // half, conv_w // GLU_PAIR

    def src_block(j):
        k = j - first
        paired = first + (k % 2) * groups + k // 2
        return jnp.where((k >= 0) & (k < 2 * groups), paired, j)

    return pl.pallas_call(
        _cast_kernel,
        grid=(W // half,),
        in_specs=[pl.BlockSpec((D, half), lambda j: (0, src_block(j)))],
        out_specs=pl.BlockSpec((D, half), lambda j: (0, j)),
        out_shape=jax.ShapeDtypeStruct((D, W), _BF16),
        compiler_params=_params("parallel"),
        name="pair_cast",
    )(w_in)


def _inproj(x2, gain, w_in, cos, sin, glu_bias, later_weights, *, seq, tm, tn):
    T, D = x2.shape
    W = w_in.shape[1]
    qk_w = N_HEADS * 2 * HEAD_DIM
    v_w = N_HEADS * V_HEAD_DIM
    conv_w = glu_bias.shape[1]
    assert qk_w % tn == 0 and v_w % tn == 0 and conv_w % tn == 0 and W % tn == 0
    assert tn % GLU_PAIR == 0
    assert seq % tm == 0 and T % tm == 0
    qk_tiles, v_tiles, conv_tiles = qk_w // tn, v_w // tn, conv_w // tn
    conv0 = 2 * qk_tiles + v_tiles
    pos_blocks = seq // tm
    ni, nj = T // tm, W // tn
    z_tile = lambda j: jnp.where(j < conv0, j, jnp.maximum(j - conv_tiles, conv0))
    glu_tile = lambda j: jnp.clip(j - conv0, 0, conv_tiles - 1)
    nchunks = 1 << ((ni * nj).bit_length() - 1)
    for w in later_weights:
        nchunks = min(nchunks, w.shape[0] // BF16_ROWS)
    chunk_map = lambda i, j: (jnp.minimum(i * nj + j, nchunks - 1), 0)
    side_specs = []
    for w in later_weights:
        assert w.shape[0] % (nchunks * BF16_ROWS) == 0
        side_specs.append(pl.BlockSpec((w.shape[0] // nchunks, w.shape[1]), chunk_map))
    kern = functools.partial(_inproj_kernel, tn=tn, qk_tiles=qk_tiles, v_tiles=v_tiles,
                             conv_tiles=conv_tiles, n_side=len(later_weights))
    outs = pl.pallas_call(
        kern,
        grid=(ni, nj),
        in_specs=[
            pl.BlockSpec((tm, D), lambda i, j: (i, 0)),
            pl.BlockSpec((1, D), lambda i, j: (0, 0)),
            pl.BlockSpec((D, tn), lambda i, j: (0, j)),
            pl.BlockSpec((tm, HEAD_DIM), lambda i, j: (i % pos_blocks, 0)),
            pl.BlockSpec((tm, HEAD_DIM), lambda i, j: (i % pos_blocks, 0)),
            pl.BlockSpec((1, tn // 2), lambda i, j: (0, glu_tile(j))),
            pl.BlockSpec((1, tn // 2), lambda i, j: (0, conv_tiles + glu_tile(j))),
        ] + side_specs,
        out_specs=[pl.BlockSpec((tm, tn), lambda i, j: (i, z_tile(j))),
                   pl.BlockSpec((tm, tn // 2), lambda i, j: (i, glu_tile(j)))] + side_specs,
        out_shape=[jax.ShapeDtypeStruct((T, W - conv_w), _BF16),
                   jax.ShapeDtypeStruct((T, conv_w // 2), _BF16)]
        + [jax.ShapeDtypeStruct(w.shape, _BF16) for w in later_weights],
        scratch_shapes=[pltpu.VMEM((tm, D), _BF16)],
        compiler_params=_params("arbitrary", "arbitrary"),
        name="inproj",
    )(x2, gain, w_in, cos, sin, glu_bias, glu_bias, *later_weights)
    return outs[0], outs[1], outs[2:]


def _attn_kernel(lq1_ref, lk1_ref, lq2_ref, lk2_ref, g_ref, q_ref, k_ref, v_ref, o_ref,
                 s_ref, a_ref, part_ref, stat_ref, *, tq, lam_init):
    S = q_ref.shape[0]
    d = HEAD_DIM
    rc = BF16_ROWS
    nq = S // tq
    lam = (jnp.exp(jnp.sum(lq1_ref[...] * lk1_ref[...], axis=-1, keepdims=True))
           - jnp.exp(jnp.sum(lq2_ref[...] * lk2_ref[...], axis=-1, keepdims=True))
           + lam_init)
    row = lax.broadcasted_iota(jnp.int32, (tq, tq), 0)
    col = lax.broadcasted_iota(jnp.int32, (tq, tq), 1)
    causal = col <= row
    contract_last = (((1,), (1,)), ((), ()))

    def scores(i):
        off = i * tq
        end = off + tq
        sl = s_ref.at[i % 2]
        q = q_ref[off:end, :]
        for h in range(2):
            qh = q[:, h * d:(h + 1) * d]
            if off:
                sl[h, :, :off] = lax.dot_general(qh, k_ref[:off, h * d:(h + 1) * d], contract_last,
                                                 preferred_element_type=_F32)
            sd = lax.dot_general(qh, k_ref[off:end, h * d:(h + 1) * d], contract_last,
                                 preferred_element_type=_F32)
            sl[h, :, off:end] = jnp.where(causal, sd, NEG_INF)

    def output(i):
        off = i * tq
        end = off + tq
        half = tq // 2
        top = jnp.dot(a_ref[:half, :end - half], v_ref[:end - half, :], preferred_element_type=_F32)
        bot = jnp.dot(a_ref[half:, :end], v_ref[:end, :], preferred_element_type=_F32)
        for r0, o in ((off, top), (off + half, bot)):
            o = _rms(o, g_ref[...], SUBLN_EPS) * (1.0 - lam_init)
            o_ref[r0:r0 + half, :] = o.astype(_BF16)

    scores(0)
    for i in range(nq):
        off = i * tq
        end = off + tq
        ncol = end // LANES
        sl = s_ref.at[i % 2]

        def lane_block(ref, h, r0, c):
            return ref[h, pl.ds(r0, rc), c * LANES:(c + 1) * LANES]

        def segments(make_body):
            for g in range(tq // LANES):
                nvalid = off // LANES + g + 1
                body = make_body(nvalid)
                for r in range(g * (LANES // rc), (g + 1) * (LANES // rc)):
                    body(r, 0)

        def max_pass(nvalid):
            def body(r, _):
                r0 = r * rc
                for h in range(2):
                    m = [lane_block(sl, h, r0, c) for c in range(min(2, nvalid))]
                    for c in range(2, nvalid):
                        m[c % 2] = jnp.maximum(m[c % 2], lane_block(sl, h, r0, c))
                    part_ref[h, pl.ds(r0, rc), :] = m[0] if nvalid == 1 else jnp.maximum(m[0], m[1])
                return 0
            return body

        segments(max_pass)
        for h in range(2):
            m = jnp.max(part_ref[h], axis=-1, keepdims=True)
            stat_ref[h] = jnp.broadcast_to(m, (tq, LANES))
        if i + 1 < nq:
            scores(i + 1)

        def exp_pass(nvalid):
            def body(r, _):
                r0 = r * rc
                for h in range(2):
                    m = stat_ref[h, pl.ds(r0, rc), :]
                    acc = None
                    for c in range(nvalid):
                        p = jnp.exp2(lane_block(sl, h, r0, c) - m)
                        sl[h, pl.ds(r0, rc), c * LANES:(c + 1) * LANES] = p
                        acc = p if acc is None else acc + p
                    part_ref[h, pl.ds(r0, rc), :] = acc
                return 0
            return body

        segments(exp_pass)
        l1 = jnp.sum(part_ref[0], axis=-1, keepdims=True)
        l2 = jnp.sum(part_ref[1], axis=-1, keepdims=True)
        stat_ref[0] = jnp.broadcast_to(1.0 / l1, (tq, LANES))
        stat_ref[1] = jnp.broadcast_to(lam / l2, (tq, LANES))
        if i > 0:
            output(i - 1)

        def combine_pass(nvalid):
            def body(r, _):
                r0 = r * rc
                c1 = stat_ref[0, pl.ds(r0, rc), :]
                c2 = stat_ref[1, pl.ds(r0, rc), :]
                for c in range(nvalid):
                    a = lane_block(sl, 0, r0, c) * c1 - lane_block(sl, 1, r0, c) * c2
                    a_ref[pl.ds(r0, rc), c * LANES:(c + 1) * LANES] = a.astype(_BF16)
                if nvalid < ncol:
                    a_ref[pl.ds(r0, rc), nvalid * LANES:end] = jnp.zeros((rc, end - nvalid * LANES), _BF16)
                return 0
            return body

        segments(combine_pass)
    output(nq - 1)


def _attention(z, lq1, lk1, lq2, lk2, subln_gain, *, batch, seq, tq, lam_init):
    T = z.shape[0]
    hw = V_HEAD_DIM
    assert seq % tq == 0 and tq % LANES == 0
    kern = functools.partial(_attn_kernel, tq=tq, lam_init=lam_init)
    vec = pl.BlockSpec((1, HEAD_DIM), lambda b, h: (0, 0))
    return pl.pallas_call(
        kern,
        grid=(batch, N_HEADS),
        in_specs=[
            vec, vec, vec, vec,
            pl.BlockSpec((1, hw), lambda b, h: (0, 0)),
            pl.BlockSpec((seq, hw), lambda b, h: (b, h)),
            pl.BlockSpec((seq, hw), lambda b, h: (b, N_HEADS + h)),
            pl.BlockSpec((seq, hw), lambda b, h: (b, 2 * N_HEADS + h)),
        ],
        out_specs=pl.BlockSpec((seq, hw), lambda b, h: (b, h)),
        out_shape=jax.ShapeDtypeStruct((T, N_HEADS * hw), _BF16),
        scratch_shapes=[pltpu.VMEM((2, 2, tq, seq), _F32), pltpu.VMEM((tq, seq), _BF16),
                        pltpu.VMEM((2, tq, LANES), _F32), pltpu.VMEM((2, tq, LANES), _F32)],
        compiler_params=_params("parallel", "parallel"),
        name="diff_attn",
    )(lq1, lk1, lq2, lk2, subln_gain, z, z, z)


def _conv_merge_kernel(glu_ref, halo_ref, w_ref, b_ref, lg_ref, lb_ref,
                       a_ref, ga_ref, gc_ref, x_ref, wc_ref, bc_ref, wo_ref, gpost_ref, gff_ref,
                       h_ref, u_ref, g_scr, y_scr, c_scr, m_scr, yo_scr, *, rc, pw, sblocks, nblk):
    t = pl.program_id(0)
    tc, C = c_scr.shape
    D = yo_scr.shape[1]

    @pl.when(t == 0)
    def _():
        c_scr[...] = jnp.zeros_like(c_scr)

    nchunk = D // pw

    def merge_chunk(n):
        cols = slice(n * pw, (n + 1) * pw)
        conv_out = jnp.dot(c_scr[...], wc_ref[:, cols], preferred_element_type=_F32) + bc_ref[:, cols]
        m = (ga_ref[:, cols].astype(_F32) * a_ref[:, cols].astype(_F32)
             + gc_ref[:, cols].astype(_F32) * conv_out)
        m_scr[:, cols] = m.astype(_BF16)

    def out_chunk(n):
        cols = slice(n * pw, (n + 1) * pw)
        yo_scr[:, cols] = jnp.dot(m_scr[...], wo_ref[:, cols], preferred_element_type=_F32)

    pieces = ([(merge_chunk, n, c_scr) for n in range(nchunk)]
              + [(out_chunk, n, m_scr) for n in range(nchunk)])

    def pace(operand_ref, produced):
        bits = pltpu.bitcast(produced[:BF16_ROWS, :], jnp.uint32)
        zero = pltpu.bitcast(lax.shift_right_logical(lax.shift_right_logical(bits, jnp.uint32(31)),
                                                     jnp.uint32(1)), _F32)
        tile = operand_ref[:BF16_ROWS, :LANES]
        operand_ref[:BF16_ROWS, :LANES] = tile + zero.astype(tile.dtype)

    seq_start = jnp.minimum(t, nblk - 1) % sblocks == 0
    g_scr[:HALO, :] = jnp.where(seq_start, 0.0, halo_ref[...].astype(_F32))
    g_scr[HALO:, :] = glu_ref[...].astype(_F32)

    base = HALO - (CONV_WIDTH - 1)
    nchan = C // LANES
    stride = nchan // len(pieces)
    assert stride >= 1
    for ci in range(nchan):
        c0 = ci * LANES
        for r0 in range(0, tc, rc):
            win = g_scr[r0:r0 + rc + HALO, c0:c0 + LANES]
            acc = None
            for r in range(SUBLANES):
                part = None
                for k in range(CONV_WIDTH):
                    if (base + k) % SUBLANES == r:
                        term = w_ref[k:k + 1, c0:c0 + LANES] * win[base + k:base + k + rc, :]
                        part = term if part is None else part + term
                acc = part if acc is None else acc + part
            y_scr[r0:r0 + rc, c0:c0 + LANES] = acc
            if ci % stride == 0:
                for s0 in range(0, rc, BF16_ROWS):
                    done = acc[s0:s0 + BF16_ROWS, :] if r0 == 0 and s0 == 0 else done + acc[s0:s0 + BF16_ROWS, :]
        if ci % stride == 0 and ci // stride < len(pieces):
            fn, n, operand_ref = pieces[ci // stride]
            pace(operand_ref, done)
            fn(n)

    h = x_ref[...] + _rms(yo_scr[...], gpost_ref[...], NORM_EPS)
    h_ref[...] = h
    u_ref[...] = _rms(h, gff_ref[...], NORM_EPS).astype(_BF16)

    yb = y_scr[...] + b_ref[...]
    mu = jnp.mean(yb, axis=-1, keepdims=True)
    yc = yb - mu
    yn = yc * lax.rsqrt(jnp.mean(yc * yc, axis=-1, keepdims=True) + LN_EPS)
    yn = yn * lg_ref[...] + lb_ref[...]
    c_scr[...] = (yn * jax.nn.sigmoid(yn)).astype(_BF16)


def _conv_merge(glu, z, attn, x2, dw_kernel, dw_bias, ln_gain, ln_bias, w_conv_out, b_conv_out,
                w_out, g_post, g_ff, *, seq, tc, ga_col, rc=64, pw=512):
    T, D = x2.shape
    C = dw_bias.shape[1]
    assert seq % tc == 0 and T % tc == 0 and tc % HALO == 0 and tc % rc == 0
    assert glu.shape == (T, C) and ga_col % D == 0 and C == D and D % pw == 0
    gb = ga_col // D
    sblocks = seq // tc
    nblk = T // tc
    hpb = tc // HALO

    conv_blk = lambda t: jnp.minimum(t, nblk - 1)
    proj_blk = lambda t: jnp.maximum(t - 1, 0)
    conv_row = pl.BlockSpec((tc, C), lambda t: (conv_blk(t), 0))
    halo = pl.BlockSpec((HALO, C), lambda t: (jnp.maximum(conv_blk(t) * hpb - 1, 0), 0))
    proj_row = lambda col: pl.BlockSpec((tc, D), lambda t: (proj_blk(t), col))
    const = lambda shape: pl.BlockSpec(shape, lambda t: (0, 0), pipeline_mode=pl.Buffered(1))
    kern = functools.partial(_conv_merge_kernel, rc=rc, pw=pw, sblocks=sblocks, nblk=nblk)
    return pl.pallas_call(
        kern,
        grid=(nblk + 1,),
        in_specs=[conv_row, halo,
                  const((CONV_WIDTH, C)), const((1, C)), const((1, C)), const((1, C)),
                  proj_row(0), proj_row(gb), proj_row(gb + 1), proj_row(0),
                  const((C, D)), const((1, D)), const((D, D)), const((1, D)), const((1, D))],
        out_specs=[proj_row(0), proj_row(0)],
        out_shape=[jax.ShapeDtypeStruct((T, D), _F32), jax.ShapeDtypeStruct((T, D), _BF16)],
        scratch_shapes=[pltpu.VMEM((tc + HALO, C), _F32), pltpu.VMEM((tc, C), _F32),
                        pltpu.VMEM((tc, C), _BF16), pltpu.VMEM((tc, D), _BF16),
                        pltpu.VMEM((tc, D), _F32)],
        compiler_params=_params("arbitrary"),
        name="conv_merge",
    )(glu, glu, dw_kernel, dw_bias, ln_gain, ln_bias, attn, z, z, x2,
      w_conv_out, b_conv_out, w_out, g_post, g_ff)


def _ffn_kernel(u_ref, w1_ref, w2_ref, h_ref, g_ref, o_ref, acc_ref):
    j = pl.program_id(1)

    @pl.when(j == 0)
    def _():
        acc_ref[...] = jnp.zeros_like(acc_ref)

    a = jnp.dot(u_ref[...], w1_ref[...], preferred_element_type=_F32)
    a = jnp.square(jnp.maximum(a, 0.0)).astype(_BF16)
    acc_ref[...] += jnp.dot(a, w2_ref[...], preferred_element_type=_F32)

    @pl.when(j == pl.num_programs(1) - 1)
    def _():
        o_ref[...] = h_ref[...] + _rms(acc_ref[...], g_ref[...], NORM_EPS)


def _ffn(u2, w1, w2, h1, gain, *, tm, tf):
    T, D = h1.shape
    F = w1.shape[1]
    assert T % tm == 0 and F % tf == 0
    return pl.pallas_call(
        _ffn_kernel,
        grid=(T // tm, F // tf),
        in_specs=[
            pl.BlockSpec((tm, D), lambda i, j: (i, 0)),
            pl.BlockSpec((D, tf), lambda i, j: (0, j)),
            pl.BlockSpec((tf, D), lambda i, j: (j, 0)),
            pl.BlockSpec((tm, D), lambda i, j: (i, 0)),
            pl.BlockSpec((1, D), lambda i, j: (0, 0)),
        ],
        out_specs=pl.BlockSpec((tm, D), lambda i, j: (i, 0)),
        out_shape=jax.ShapeDtypeStruct((T, D), _F32),
        scratch_shapes=[pltpu.VMEM((tm, D), _F32)],
        compiler_params=_params("parallel", "arbitrary"),
        name="ffn",
    )(u2, w1, w2, h1, gain)


def _rope_tables(seq):
    d = HEAD_DIM
    inv_freq = 1.0 / (ROPE_THETA ** (jnp.arange(0, d, 2, dtype=_F32) / d))
    ang = jnp.arange(seq, dtype=_F32)[:, None] * inv_freq[None, :]
    cos = jnp.cos(ang)
    sin = jnp.sin(ang)
    return jnp.concatenate([cos, cos], axis=-1), jnp.concatenate([-sin, sin], axis=-1)


def _layer(h2d, l, p, *, batch, seq):
    D = h2d.shape[1]
    lam_init = 0.8 - 0.6 * math.exp(-0.3 * l)
    row = lambda a: a[l][None, :].astype(_F32)
    cos, sin = _rope_tables(seq)
    qk_w = N_HEADS * 2 * HEAD_DIM
    conv_col = 2 * qk_w + N_HEADS * V_HEAD_DIM
    conv_w = p['glu_bias'].shape[1]

    tm_in = min(1024, seq)
    tn_in = 1024
    w_in = _paired_bf16(p['w_in'][l].astype(_F32), conv_col, conv_w)
    later = [p[name][l].astype(_F32) for name in ('w_conv_out', 'w_out', 'w_ff1', 'w_ff2')]
    z, glu, (w_conv_out, w_out, w_ff1, w_ff2) = _inproj(
        h2d, row(p['pre_mix_gain']), w_in, cos, sin, row(p['glu_bias']), later,
        seq=seq, tm=tm_in, tn=tn_in)
    attn = _attention(z, row(p['lambda_q1']), row(p['lambda_k1']), row(p['lambda_q2']),
                      row(p['lambda_k2']), row(p['subln_gain']),
                      batch=batch, seq=seq, tq=min(512, seq), lam_init=lam_init)
    h1, u2 = _conv_merge(glu, z, attn, h2d, p['dw_kernel'][l].astype(_F32), row(p['dw_bias']),
                         row(p['conv_ln_gain']), row(p['conv_ln_bias']),
                         w_conv_out, row(p['b_conv_out']),
                         w_out, row(p['post_mix_gain']),
                         row(p['pre_ff_gain']), seq=seq, tc=min(256, seq), ga_col=conv_col)
    return _ffn(u2, w_ff1, w_ff2, h1,
                row(p['post_ff_gain']), tm=min(512, seq), tf=1024)


def kernel(x, pre_mix_gain, w_in, lambda_q1, lambda_k1, lambda_q2, lambda_k2, subln_gain, glu_bias,
           dw_kernel, dw_bias, conv_ln_gain, conv_ln_bias, w_conv_out, b_conv_out, w_out,
           post_mix_gain, pre_ff_gain, w_ff1, w_ff2, post_ff_gain):
    B, S, D = x.shape
    p = dict(pre_mix_gain=pre_mix_gain, w_in=w_in, lambda_q1=lambda_q1, lambda_k1=lambda_k1,
             lambda_q2=lambda_q2, lambda_k2=lambda_k2, subln_gain=subln_gain, glu_bias=glu_bias,
             dw_kernel=dw_kernel, dw_bias=dw_bias, conv_ln_gain=conv_ln_gain,
             conv_ln_bias=conv_ln_bias, w_conv_out=w_conv_out, b_conv_out=b_conv_out, w_out=w_out,
             post_mix_gain=post_mix_gain, pre_ff_gain=pre_ff_gain, w_ff1=w_ff1, w_ff2=w_ff2,
             post_ff_gain=post_ff_gain)
    h = x.reshape(B * S, D).astype(_F32)
    for l in range(w_in.shape[0]):
        h = _layer(h, l, p, batch=B, seq=S)
    return h.reshape(B, S, D).astype(x.dtype)
```

```python
import functools
import math

import jax
import jax.numpy as jnp
from jax import lax
from jax.experimental import pallas as pl
from jax.experimental.pallas import tpu as pltpu

N_HEADS = 8
HEAD_DIM = 128
V_HEAD_DIM = 2 * HEAD_DIM
ROPE_THETA = 10000.0
SUBLN_EPS = 1e-5
CONV_WIDTH = 31
LN_EPS = 1e-5
NORM_EPS = 1e-6
NEG_INF = -1e30
LOG2_E = 1.4426950408889634

LANES = 128
SUBLANES = 8
BF16_ROWS = 16
MXU_COLS = 256
GLU_PAIR = 2 * MXU_COLS
VMEM_LIMIT_BYTES = 56 * 1024 * 1024
HALO = 32

_BF16 = jnp.bfloat16
_F32 = jnp.float32


def _params(*semantics):
    return pltpu.CompilerParams(dimension_semantics=semantics, vmem_limit_bytes=VMEM_LIMIT_BYTES)


def _rms(v, gain, eps):
    return v * lax.rsqrt(jnp.mean(v * v, axis=-1, keepdims=True) + eps) * gain


def _inproj_kernel(x_ref, g_ref, w_ref, cos_ref, sin_ref, bv_ref, bg_ref, *rest,
                   tn, qk_tiles, v_tiles, conv_tiles, n_side):
    side_in = rest[:n_side]
    z_ref, glu_ref = rest[n_side:n_side + 2]
    side_out = rest[n_side + 2:2 * n_side + 2]
    u_ref = rest[2 * n_side + 2]
    j = pl.program_id(1)

    @pl.when(j == 0)
    def _():
        u_ref[...] = _rms(x_ref[...], g_ref[...], NORM_EPS).astype(_BF16)

    def matmul():
        for src, dst in zip(side_in, side_out):
            dst[...] = src[...].astype(_BF16)
        return jnp.dot(u_ref[...], w_ref[...], preferred_element_type=_F32)

    def rope_store(scale):
        acc = matmul()
        cos = cos_ref[...]
        sin = sin_ref[...]
        for c in range(tn // HEAD_DIM):
            sl = slice(c * HEAD_DIM, (c + 1) * HEAD_DIM)
            a = acc[:, sl]
            r = a * cos + pltpu.roll(a, HEAD_DIM // 2, axis=1) * sin
            if scale is not None:
                r = r * scale
            z_ref[:, sl] = r.astype(_BF16)

    def sigmoid(a):
        return 0.5 * jnp.tanh(0.5 * a) + 0.5

    q_end = qk_tiles
    k_end = 2 * qk_tiles
    v_end = k_end + v_tiles
    c_end = v_end + conv_tiles

    @pl.when(j < q_end)
    def _():
        rope_store(HEAD_DIM ** -0.5 * LOG2_E)

    @pl.when((j >= q_end) & (j < k_end))
    def _():
        rope_store(None)

    @pl.when((j >= k_end) & (j < v_end))
    def _():
        z_ref[...] = matmul().astype(_BF16)

    @pl.when((j >= v_end) & (j < c_end))
    def _():
        acc = matmul()
        half = GLU_PAIR // 2
        for c in range(tn // GLU_PAIR):
            out_cols = slice(c * half, (c + 1) * half)
            val = acc[:, c * GLU_PAIR:c * GLU_PAIR + half] + bv_ref[:, out_cols]
            gate = acc[:, c * GLU_PAIR + half:(c + 1) * GLU_PAIR] + bg_ref[:, out_cols]
            glu_ref[:, out_cols] = (val * sigmoid(gate)).astype(_BF16)

    @pl.when(j >= c_end)
    def _():
        z_ref[...] = sigmoid(matmul()).astype(_BF16)


def _cast_kernel(src_ref, dst_ref):
    dst_ref[...] = src_ref[...].astype(dst_ref.dtype)


def _paired_bf16(w_in, conv_col, conv_w):
    D, W = w_in.shape
    half = GLU_PAIR // 2
    assert W % half == 0 and conv_col % half == 0 and conv_w % GLU_PAIR == 0
    first, groups = conv_col // half, conv_w // GLU_PAIR

    def src_block(j):
        k = j - first
        paired = first + (k % 2) * groups + k // 2
        return jnp.where((k >= 0) & (k < 2 * groups), paired, j)

    return pl.pallas_call(
        _cast_kernel,
        grid=(W // half,),
        in_specs=[pl.BlockSpec((D, half), lambda j: (0, src_block(j)))],
        out_specs=pl.BlockSpec((D, half), lambda j: (0, j)),
        out_shape=jax.ShapeDtypeStruct((D, W), _BF16),
        compiler_params=_params("parallel"),
        name="pair_cast",
    )(w_in)


def _inproj(x2, gain, w_in, cos, sin, glu_bias, later_weights, *, seq, tm, tn):
    T, D = x2.shape
    W = w_in.shape[1]
    qk_w = N_HEADS * 2 * HEAD_DIM
    v_w = N_HEADS * V_HEAD_DIM
    conv_w = glu_bias.shape[1]
    assert qk_w % tn == 0 and v_w % tn == 0 and conv_w % tn == 0 and W % tn == 0
    assert tn % GLU_PAIR == 0
    assert seq % tm == 0 and T % tm == 0
    qk_tiles, v_tiles, conv_tiles = qk_w // tn, v_w // tn, conv_w // tn
    conv0 = 2 * qk_tiles + v_tiles
    pos_blocks = seq // tm
    ni, nj = T // tm, W // tn
    z_tile = lambda j: jnp.where(j < conv0, j, jnp.maximum(j - conv_tiles, conv0))
    glu_tile = lambda j: jnp.clip(j - conv0, 0, conv_tiles - 1)
    nchunks = 1 << ((ni * nj).bit_length() - 1)
    for w in later_weights:
        nchunks = min(nchunks, w.shape[0] // BF16_ROWS)
    chunk_map = lambda i, j: (jnp.minimum(i * nj + j, nchunks - 1), 0)
    side_specs = []
    for w in later_weights:
        assert w.shape[0] % (nchunks * BF16_ROWS) == 0
        side_specs.append(pl.BlockSpec((w.shape[0] // nchunks, w.shape[1]), chunk_map))
    kern = functools.partial(_inproj_kernel, tn=tn, qk_tiles=qk_tiles, v_tiles=v_tiles,
                             conv_tiles=conv_tiles, n_side=len(later_weights))
    outs = pl.pallas_call(
        kern,
        grid=(ni, nj),
        in_specs=[
            pl.BlockSpec((tm, D), lambda i, j: (i, 0)),
            pl.BlockSpec((1, D), lambda i, j: (0, 0)),
            pl.BlockSpec((D, tn), lambda i, j: (0, j)),
            pl.BlockSpec((tm, HEAD_DIM), lambda i, j: (i % pos_blocks, 0)),
            pl.BlockSpec((tm, HEAD_DIM), lambda i, j: (i % pos_blocks, 0)),
            pl.BlockSpec((1, tn // 2), lambda i, j: (0, glu_tile(j))),
            pl.BlockSpec((1, tn // 2), lambda i, j: (0, conv_tiles + glu_tile(j))),
        ] + side_specs,
        out_specs=[pl.BlockSpec((tm, tn), lambda i, j: (i, z_tile(j))),
                   pl.BlockSpec((tm, tn // 2), lambda i, j: (i, glu_tile(j)))] + side_specs,
        out_shape=[jax.ShapeDtypeStruct((T, W - conv_w), _BF16),
                   jax.ShapeDtypeStruct((T, conv_w // 2), _BF16)]
        + [jax.ShapeDtypeStruct(w.shape, _BF16) for w in later_weights],
        scratch_shapes=[pltpu.VMEM((tm, D), _BF16)],
        compiler_params=_params("arbitrary", "arbitrary"),
        name="inproj",
    )(x2, gain, w_in, cos, sin, glu_bias, glu_bias, *later_weights)
    return outs[0], outs[1], outs[2:]


def _attn_kernel(lq1_ref, lk1_ref, lq2_ref, lk2_ref, g_ref, q_ref, k_ref, v_ref, o_ref,
                 s_ref, a_ref, part_ref, stat_ref, *, tq, lam_init):
    S = q_ref.shape[0]
    d = HEAD_DIM
    rc = BF16_ROWS
    nq = S // tq
    lam = (jnp.exp(jnp.sum(lq1_ref[...] * lk1_ref[...], axis=-1, keepdims=True))
           - jnp.exp(jnp.sum(lq2_ref[...] * lk2_ref[...], axis=-1, keepdims=True))
           + lam_init)
    row = lax.broadcasted_iota(jnp.int32, (tq, tq), 0)
    col = lax.broadcasted_iota(jnp.int32, (tq, tq), 1)
    causal = col <= row
    contract_last = (((1,), (1,)), ((), ()))

    def scores(i):
        off = i * tq
        end = off + tq
        sl = s_ref.at[i % 2]
        q = q_ref[off:end, :]
        for h in range(2):
            qh = q[:, h * d:(h + 1) * d]
            if off:
                sl[h, :, :off] = lax.dot_general(qh, k_ref[:off, h * d:(h + 1) * d], contract_last,
                                                 preferred_element_type=_F32)
            sd = lax.dot_general(qh, k_ref[off:end, h * d:(h + 1) * d], contract_last,
                                 preferred_element_type=_F32)
            sl[h, :, off:end] = jnp.where(causal, sd, NEG_INF)

    def output(i):
        off = i * tq
        end = off + tq
        half = tq // 2
        top = jnp.dot(a_ref[:half, :end - half], v_ref[:end - half, :], preferred_element_type=_F32)
        bot = jnp.dot(a_ref[half:, :end], v_ref[:end, :], preferred_element_type=_F32)
        for r0, o in ((off, top), (off + half, bot)):
            o = _rms(o, g_ref[...], SUBLN_EPS) * (1.0 - lam_init)
            o_ref[r0:r0 + half, :] = o.astype(_BF16)

    scores(0)
    for i in range(nq):
        off = i * tq
        end = off + tq
        ncol = end // LANES
        sl = s_ref.at[i % 2]

        def lane_block(ref, h, r0, c):
            return ref[h, pl.ds(r0, rc), c * LANES:(c + 1) * LANES]

        def segments(make_body):
            for g in range(tq // LANES):
                nvalid = off // LANES + g + 1
                body = make_body(nvalid)
                for r in range(g * (LANES // rc), (g + 1) * (LANES // rc)):
                    body(r, 0)

        def max_pass(nvalid):
            def body(r, _):
                r0 = r * rc
                for h in range(2):
                    m = [lane_block(sl, h, r0, c) for c in range(min(2, nvalid))]
                    for c in range(2, nvalid):
                        m[c % 2] = jnp.maximum(m[c % 2], lane_block(sl, h, r0, c))
                    part_ref[h, pl.ds(r0, rc), :] = m[0] if nvalid == 1 else jnp.maximum(m[0], m[1])
                return 0
            return body

        segments(max_pass)
        for h in range(2):
            m = jnp.max(part_ref[h], axis=-1, keepdims=True)
            stat_ref[h] = jnp.broadcast_to(m, (tq, LANES))
        if i + 1 < nq:
            scores(i + 1)

        def exp_pass(nvalid):
            def body(r, _):
                r0 = r * rc
                for h in range(2):
                    m = stat_ref[h, pl.ds(r0, rc), :]
                    acc = None
                    for c in range(nvalid):
                        p = jnp.exp2(lane_block(sl, h, r0, c) - m)
                        sl[h, pl.ds(r0, rc), c * LANES:(c + 1) * LANES] = p
                        acc = p if acc is None else acc + p
                    part_ref[h, pl.ds(r0, rc), :] = acc
                return 0
            return body

        segments(exp_pass)
        l1 = jnp.sum(part_ref[0], axis=-1, keepdims=True)
        l2 = jnp.sum(part_ref[1], axis=-1, keepdims=True)
        stat_ref[0] = jnp.broadcast_to(1.0 / l1, (tq, LANES))
        stat_ref[1] = jnp.broadcast_to(lam / l2, (tq, LANES))
        if i > 0:
            output(i - 1)

        def combine_pass(nvalid):
            def body(r, _):
                r0 = r * rc
                c1 = stat_ref[0, pl.ds(r0, rc), :]
                c2 = stat_ref[1, pl.ds(r0, rc), :]
                for c in range(nvalid):
                    a = lane_block(sl, 0, r0, c) * c1 - lane_block(sl, 1, r0, c) * c2
                    a_ref[pl.ds(r0, rc), c * LANES:(c + 1) * LANES] = a.astype(_BF16)
                if nvalid < ncol:
                    a_ref[pl.ds(r0, rc), nvalid * LANES:end] = jnp.zeros((rc, end - nvalid * LANES), _BF16)
                return 0
            return body

        segments(combine_pass)
    output(nq - 1)


def _attention(z, lq1, lk1, lq2, lk2, subln_gain, *, batch, seq, tq, lam_init):
    T = z.shape[0]
    hw = V_HEAD_DIM
    assert seq % tq == 0 and tq % LANES == 0
    kern = functools.partial(_attn_kernel, tq=tq, lam_init=lam_init)
    vec = pl.BlockSpec((1, HEAD_DIM), lambda b, h: (0, 0))
    return pl.pallas_call(
        kern,
        grid=(batch, N_HEADS),
        in_specs=[
            vec, vec, vec, vec,
            pl.BlockSpec((1, hw), lambda b, h: (0, 0)),
            pl.BlockSpec((seq, hw), lambda b, h: (b, h)),
            pl.BlockSpec((seq, hw), lambda b, h: (b, N_HEADS + h)),
            pl.BlockSpec((seq, hw), lambda b, h: (b, 2 * N_HEADS + h)),
        ],
        out_specs=pl.BlockSpec((seq, hw), lambda b, h: (b, h)),
        out_shape=jax.ShapeDtypeStruct((T, N_HEADS * hw), _BF16),
        scratch_shapes=[pltpu.VMEM((2, 2, tq, seq), _F32), pltpu.VMEM((tq, seq), _BF16),
                        pltpu.VMEM((2, tq, LANES), _F32), pltpu.VMEM((2, tq, LANES), _F32)],
        compiler_params=_params("parallel", "parallel"),
        name="diff_attn",
    )(lq1, lk1, lq2, lk2, subln_gain, z, z, z)


def _conv_merge_kernel(glu_ref, halo_ref, w_ref, b_ref, lg_ref, lb_ref,
                       a_ref, ga_ref, gc_ref, x_ref, wc_ref, bc_ref, wo_ref, gpost_ref, gff_ref,
                       h_ref, u_ref, g_scr, y_scr, c_scr, m_scr, yo_scr, *, rc, pw, sblocks, nblk):
    t = pl.program_id(0)
    tc, C = c_scr.shape
    D = yo_scr.shape[1]

    @pl.when(t == 0)
    def _():
        c_scr[...] = jnp.zeros_like(c_scr)

    nchunk = D // pw

    def merge_chunk(n):
        cols = slice(n * pw, (n + 1) * pw)
        conv_out = jnp.dot(c_scr[...], wc_ref[:, cols], preferred_element_type=_F32) + bc_ref[:, cols]
        m = (ga_ref[:, cols].astype(_F32) * a_ref[:, cols].astype(_F32)
             + gc_ref[:, cols].astype(_F32) * conv_out)
        m_scr[:, cols] = m.astype(_BF16)

    def out_chunk(n):
        cols = slice(n * pw, (n + 1) * pw)
        yo_scr[:, cols] = jnp.dot(m_scr[...], wo_ref[:, cols], preferred_element_type=_F32)

    pieces = ([(merge_chunk, n, c_scr) for n in range(nchunk)]
              + [(out_chunk, n, m_scr) for n in range(nchunk)])

    def pace(operand_ref, produced):
        bits = pltpu.bitcast(produced[:BF16_ROWS, :], jnp.uint32)
        zero = pltpu.bitcast(lax.shift_right_logical(lax.shift_right_logical(bits, jnp.uint32(31)),
                                                     jnp.uint32(1)), _F32)
        tile = operand_ref[:BF16_ROWS, :LANES]
        operand_ref[:BF16_ROWS, :LANES] = tile + zero.astype(tile.dtype)

    seq_start = jnp.minimum(t, nblk - 1) % sblocks == 0
    g_scr[:HALO, :] = jnp.where(seq_start, 0.0, halo_ref[...].astype(_F32))
    g_scr[HALO:, :] = glu_ref[...].astype(_F32)

    base = HALO - (CONV_WIDTH - 1)
    nchan = C // LANES
    stride = nchan // len(pieces)
    assert stride >= 1
    for ci in range(nchan):
        c0 = ci * LANES
        for r0 in range(0, tc, rc):
            win = g_scr[r0:r0 + rc + HALO, c0:c0 + LANES]
            acc = None
            for r in range(SUBLANES):
                part = None
                for k in range(CONV_WIDTH):
                    if (base + k) % SUBLANES == r:
                        term = w_ref[k:k + 1, c0:c0 + LANES] * win[base + k:base + k + rc, :]
                        part = term if part is None else part + term
                acc = part if acc is None else acc + part
            y_scr[r0:r0 + rc, c0:c0 + LANES] = acc
            if ci % stride == 0:
                for s0 in range(0, rc, BF16_ROWS):
                    done = acc[s0:s0 + BF16_ROWS, :] if r0 == 0 and s0 == 0 else done + acc[s0:s0 + BF16_ROWS, :]
        if ci % stride == 0 and ci // stride < len(pieces):
            fn, n, operand_ref = pieces[ci // stride]
            pace(operand_ref, done)
            fn(n)

    h = x_ref[...] + _rms(yo_scr[...], gpost_ref[...], NORM_EPS)
    h_ref[...] = h
    u_ref[...] = _rms(h, gff_ref[...], NORM_EPS).astype(_BF16)

    yb = y_scr[...] + b_ref[...]
    mu = jnp.mean(yb, axis=-1, keepdims=True)
    yc = yb - mu
    yn = yc * lax.rsqrt(jnp.mean(yc * yc, axis=-1, keepdims=True) + LN_EPS)
    yn = yn * lg_ref[...] + lb_ref[...]
    c_scr[...] = (yn * jax.nn.sigmoid(yn)).astype(_BF16)


def _conv_merge(glu, z, attn, x2, dw_kernel, dw_bias, ln_gain, ln_bias, w_conv_out, b_conv_out,
                w_out, g_post, g_ff, *, seq, tc, ga_col, rc=64, pw=256):
    T, D = x2.shape
    C = dw_bias.shape[1]
    assert seq % tc == 0 and T % tc == 0 and tc % HALO == 0 and tc % rc == 0
    assert glu.shape == (T, C) and ga_col % D == 0 and C == D and D % pw == 0
    gb = ga_col // D
    sblocks = seq // tc
    nblk = T // tc
    hpb = tc // HALO

    conv_blk = lambda t: jnp.minimum(t, nblk - 1)
    proj_blk = lambda t: jnp.maximum(t - 1, 0)
    conv_row = pl.BlockSpec((tc, C), lambda t: (conv_blk(t), 0))
    halo = pl.BlockSpec((HALO, C), lambda t: (jnp.maximum(conv_blk(t) * hpb - 1, 0), 0))
    proj_row = lambda col: pl.BlockSpec((tc, D), lambda t: (proj_blk(t), col))
    const = lambda shape: pl.BlockSpec(shape, lambda t: (0, 0), pipeline_mode=pl.Buffered(1))
    kern = functools.partial(_conv_merge_kernel, rc=rc, pw=pw, sblocks=sblocks, nblk=nblk)
    return pl.pallas_call(
        kern,
        grid=(nblk + 1,),
        in_specs=[conv_row, halo,
                  const((CONV_WIDTH, C)), const((1, C)), const((1, C)), const((1, C)),
                  proj_row(0), proj_row(gb), proj_row(gb + 1), proj_row(0),
                  const((C, D)), const((1, D)), const((D, D)), const((1, D)), const((1, D))],
        out_specs=[proj_row(0), proj_row(0)],
        out_shape=[jax.ShapeDtypeStruct((T, D), _F32), jax.ShapeDtypeStruct((T, D), _BF16)],
        scratch_shapes=[pltpu.VMEM((tc + HALO, C), _F32), pltpu.VMEM((tc, C), _F32),
                        pltpu.VMEM((tc, C), _BF16), pltpu.VMEM((tc, D), _BF16),
                        pltpu.VMEM((tc, D), _F32)],
        compiler_params=_params("arbitrary"),
        name="conv_merge",
    )(glu, glu, dw_kernel, dw_bias, ln_gain, ln_bias, attn, z, z, x2,
      w_conv_out, b_conv_out, w_out, g_post, g_ff)


def _ffn_kernel(u_ref, w1_ref, w2_ref, h_ref, g_ref, o_ref, acc_ref):
    j = pl.program_id(1)

    @pl.when(j == 0)
    def _():
        acc_ref[...] = jnp.zeros_like(acc_ref)

    a = jnp.dot(u_ref[...], w1_ref[...], preferred_element_type=_F32)
    a = jnp.square(jnp.maximum(a, 0.0)).astype(_BF16)
    acc_ref[...] += jnp.dot(a, w2_ref[...], preferred_element_type=_F32)

    @pl.when(j == pl.num_programs(1) - 1)
    def _():
        o_ref[...] = h_ref[...] + _rms(acc_ref[...], g_ref[...], NORM_EPS)


def _ffn(u2, w1, w2, h1, gain, *, tm, tf):
    T, D = h1.shape
    F = w1.shape[1]
    assert T % tm == 0 and F % tf == 0
    return pl.pallas_call(
        _ffn_kernel,
        grid=(T // tm, F // tf),
        in_specs=[
            pl.BlockSpec((tm, D), lambda i, j: (i, 0)),
            pl.BlockSpec((D, tf), lambda i, j: (0, j)),
            pl.BlockSpec((tf, D), lambda i, j: (j, 0)),
            pl.BlockSpec((tm, D), lambda i, j: (i, 0)),
            pl.BlockSpec((1, D), lambda i, j: (0, 0)),
        ],
        out_specs=pl.BlockSpec((tm, D), lambda i, j: (i, 0)),
        out_shape=jax.ShapeDtypeStruct((T, D), _F32),
        scratch_shapes=[pltpu.VMEM((tm, D), _F32)],
        compiler_params=_params("parallel", "arbitrary"),
        name="ffn",
    )(u2, w1, w2, h1, gain)


def _rope_tables(seq):
    d = HEAD_DIM
    inv_freq = 1.0 / (ROPE_THETA ** (jnp.arange(0, d, 2, dtype=_F32) / d))
    ang = jnp.arange(seq, dtype=_F32)[:, None] * inv_freq[None, :]
    cos = jnp.cos(ang)
    sin = jnp.sin(ang)
    return jnp.concatenate([cos, cos], axis=-1), jnp.concatenate([-sin, sin], axis=-1)


def _layer(h2d, l, p, *, batch, seq):
    D = h2d.shape[1]
    lam_init = 0.8 - 0.6 * math.exp(-0.3 * l)
    row = lambda a: a[l][None, :].astype(_F32)
    cos, sin = _rope_tables(seq)
    qk_w = N_HEADS * 2 * HEAD_DIM
    conv_col = 2 * qk_w + N_HEADS * V_HEAD_DIM
    conv_w = p['glu_bias'].shape[1]

    tm_in = min(1024, seq)
    tn_in = 1024
    w_in = _paired_bf16(p['w_in'][l].astype(_F32), conv_col, conv_w)
    later = [p[name][l].astype(_F32) for name in ('w_conv_out', 'w_out', 'w_ff1', 'w_ff2')]
    z, glu, (w_conv_out, w_out, w_ff1, w_ff2) = _inproj(
        h2d, row(p['pre_mix_gain']), w_in, cos, sin, row(p['glu_bias']), later,
        seq=seq, tm=tm_in, tn=tn_in)
    attn = _attention(z, row(p['lambda_q1']), row(p['lambda_k1']), row(p['lambda_q2']),
                      row(p['lambda_k2']), row(p['subln_gain']),
                      batch=batch, seq=seq, tq=min(512, seq), lam_init=lam_init)
    h1, u2 = _conv_merge(glu, z, attn, h2d, p['dw_kernel'][l].astype(_F32), row(p['dw_bias']),
                         row(p['conv_ln_gain']), row(p['conv_ln_bias']),
                         w_conv_out, row(p['b_conv_out']),
                         w_out, row(p['post_mix_gain']),
                         row(p['pre_ff_gain']), seq=seq, tc=min(256, seq), ga_col=conv_col)
    return _ffn(u2, w_ff1, w_ff2, h1,
                row(p['post_ff_gain']), tm=min(512, seq), tf=1024)


def kernel(x, pre_mix_gain, w_in, lambda_q1, lambda_k1, lambda_q2, lambda_k2, subln_gain, glu_bias,
           dw_kernel, dw_bias, conv_ln_gain, conv_ln_bias, w_conv_out, b_conv_out, w_out,
           post_mix_gain, pre_ff_gain, w_ff1, w_ff2, post_ff_gain):
    B, S, D = x.shape
    p = dict(pre_mix_gain=pre_mix_gain, w_in=w_in, lambda_q1=lambda_q1, lambda_k1=lambda_k1,
             lambda_q2=lambda_q2, lambda_k2=lambda_k2, subln_gain=subln_gain, glu_bias=glu_bias,
             dw_kernel=dw_kernel, dw_bias=dw_bias, conv_ln_gain=conv_ln_gain,
             conv_ln_bias=conv_ln_bias, w_conv_out=w_conv_out, b_conv_out=b_conv_out, w_out=w_out,
             post_mix_gain=post_mix_gain, pre_ff_gain=pre_ff_gain, w_ff1=w_ff1, w_ff2=w_ff2,
             post_ff_gain=post_ff_gain)
    h = x.reshape(B * S, D).astype(_F32)
    for l in range(w_in.shape[0]):
        h = _layer(h, l, p, batch=B, seq=S)
    return h.reshape(B, S, D).astype(x.dtype)
```

```python
import functools
import math

import jax
import jax.numpy as jnp
from jax import lax
from jax.experimental import pallas as pl
from jax.experimental.pallas import tpu as pltpu

N_HEADS = 8
HEAD_DIM = 128
V_HEAD_DIM = 2 * HEAD_DIM
ROPE_THETA = 10000.0
SUBLN_EPS = 1e-5
CONV_WIDTH = 31
LN_EPS = 1e-5
NORM_EPS = 1e-6
NEG_INF = -1e30
LOG2_E = 1.4426950408889634

LANES = 128
SUBLANES = 8
BF16_ROWS = 16
MXU_COLS = 256
GLU_PAIR = 2 * MXU_COLS
VMEM_LIMIT_BYTES = 56 * 1024 * 1024
HALO = 32

_BF16 = jnp.bfloat16
_F32 = jnp.float32


def _params(*semantics):
    return pltpu.CompilerParams(dimension_semantics=semantics, vmem_limit_bytes=VMEM_LIMIT_BYTES)


def _rms(v, gain, eps):
    return v * lax.rsqrt(jnp.mean(v * v, axis=-1, keepdims=True) + eps) * gain


def _inproj_kernel(x_ref, g_ref, w_ref, cos_ref, sin_ref, bv_ref, bg_ref, *rest,
                   tn, qk_tiles, v_tiles, conv_tiles, n_side):
    side_in = rest[:n_side]
    z_ref, glu_ref = rest[n_side:n_side + 2]
    side_out = rest[n_side + 2:2 * n_side + 2]
    u_ref = rest[2 * n_side + 2]
    j = pl.program_id(1)

    @pl.when(j == 0)
    def _():
        u_ref[...] = _rms(x_ref[...], g_ref[...], NORM_EPS).astype(_BF16)

    def matmul():
        for src, dst in zip(side_in, side_out):
            dst[...] = src[...].astype(_BF16)
        return jnp.dot(u_ref[...], w_ref[...], preferred_element_type=_F32)

    def rope_store(scale):
        acc = matmul()
        cos = cos_ref[...]
        sin = sin_ref[...]
        for c in range(tn // HEAD_DIM):
            sl = slice(c * HEAD_DIM, (c + 1) * HEAD_DIM)
            a = acc[:, sl]
            r = a * cos + pltpu.roll(a, HEAD_DIM // 2, axis=1) * sin
            if scale is not None:
                r = r * scale
            z_ref[:, sl] = r.astype(_BF16)

    def sigmoid(a):
        return 0.5 * jnp.tanh(0.5 * a) + 0.5

    q_end = qk_tiles
    k_end = 2 * qk_tiles
    v_end = k_end + v_tiles
    c_end = v_end + conv_tiles

    @pl.when(j < q_end)
    def _():
        rope_store(HEAD_DIM ** -0.5 * LOG2_E)

    @pl.when((j >= q_end) & (j < k_end))
    def _():
        rope_store(None)

    @pl.when((j >= k_end) & (j < v_end))
    def _():
        z_ref[...] = matmul().astype(_BF16)

    @pl.when((j >= v_end) & (j < c_end))
    def _():
        acc = matmul()
        half = GLU_PAIR // 2
        for c in range(tn // GLU_PAIR):
            out_cols = slice(c * half, (c + 1) * half)
            val = acc[:, c * GLU_PAIR:c * GLU_PAIR + half] + bv_ref[:, out_cols]
            gate = acc[:, c * GLU_PAIR + half:(c + 1) * GLU_PAIR] + bg_ref[:, out_cols]
            glu_ref[:, out_cols] = (val * sigmoid(gate)).astype(_BF16)

    @pl.when(j >= c_end)
    def _():
        z_ref[...] = sigmoid(matmul()).astype(_BF16)


def _cast_kernel(left_ref, right_ref, dst_ref):
    half = left_ref.shape[1]
    dst_ref[:, :half] = left_ref[...].astype(dst_ref.dtype)
    dst_ref[:, half:] = right_ref[...].astype(dst_ref.dtype)


def _paired_bf16(w_in, conv_col, conv_w):
    D, W = w_in.shape
    half = GLU_PAIR // 2
    assert W % GLU_PAIR == 0 and conv_col % GLU_PAIR == 0 and conv_w % GLU_PAIR == 0
    first, groups = conv_col // half, conv_w // GLU_PAIR

    def src_block(j):
        k = j - first
        paired = first + (k % 2) * groups + k // 2
        return jnp.where((k >= 0) & (k < 2 * groups), paired, j)

    return pl.pallas_call(
        _cast_kernel,
        grid=(W // GLU_PAIR,),
        in_specs=[pl.BlockSpec((D, half), lambda j: (0, src_block(2 * j))),
                  pl.BlockSpec((D, half), lambda j: (0, src_block(2 * j + 1)))],
        out_specs=pl.BlockSpec((D, GLU_PAIR), lambda j: (0, j)),
        out_shape=jax.ShapeDtypeStruct((D, W), _BF16),
        compiler_params=_params("parallel"),
        name="pair_cast",
    )(w_in, w_in)


def _inproj(x2, gain, w_in, cos, sin, glu_bias, later_weights, *, seq, tm, tn):
    T, D = x2.shape
    W = w_in.shape[1]
    qk_w = N_HEADS * 2 * HEAD_DIM
    v_w = N_HEADS * V_HEAD_DIM
    conv_w = glu_bias.shape[1]
    assert qk_w % tn == 0 and v_w % tn == 0 and conv_w % tn == 0 and W % tn == 0
    assert tn % GLU_PAIR == 0
    assert seq % tm == 0 and T % tm == 0
    qk_tiles, v_tiles, conv_tiles = qk_w // tn, v_w // tn, conv_w // tn
    conv0 = 2 * qk_tiles + v_tiles
    pos_blocks = seq // tm
    ni, nj = T // tm, W // tn
    z_tile = lambda j: jnp.where(j < conv0, j, jnp.maximum(j - conv_tiles, conv0))
    glu_tile = lambda j: jnp.clip(j - conv0, 0, conv_tiles - 1)
    nchunks = 1 << ((ni * nj).bit_length() - 1)
    for w in later_weights:
        nchunks = min(nchunks, w.shape[0] // BF16_ROWS)
    chunk_map = lambda i, j: (jnp.minimum(i * nj + j, nchunks - 1), 0)
    side_specs = []
    for w in later_weights:
        assert w.shape[0] % (nchunks * BF16_ROWS) == 0
        side_specs.append(pl.BlockSpec((w.shape[0] // nchunks, w.shape[1]), chunk_map))
    kern = functools.partial(_inproj_kernel, tn=tn, qk_tiles=qk_tiles, v_tiles=v_tiles,
                             conv_tiles=conv_tiles, n_side=len(later_weights))
    outs = pl.pallas_call(
        kern,
        grid=(ni, nj),
        in_specs=[
            pl.BlockSpec((tm, D), lambda i, j: (i, 0)),
            pl.BlockSpec((1, D), lambda i, j: (0, 0)),
            pl.BlockSpec((D, tn), lambda i, j: (0, j)),
            pl.BlockSpec((tm, HEAD_DIM), lambda i, j: (i % pos_blocks, 0)),
            pl.BlockSpec((tm, HEAD_DIM), lambda i, j: (i % pos_blocks, 0)),
            pl.BlockSpec((1, tn // 2), lambda i, j: (0, glu_tile(j))),
            pl.BlockSpec((1, tn // 2), lambda i, j: (0, conv_tiles + glu_tile(j))),
        ] + side_specs,
        out_specs=[pl.BlockSpec((tm, tn), lambda i, j: (i, z_tile(j))),
                   pl.BlockSpec((tm, tn // 2), lambda i, j: (i, glu_tile(j)))] + side_specs,
        out_shape=[jax.ShapeDtypeStruct((T, W - conv_w), _BF16),
                   jax.ShapeDtypeStruct((T, conv_w // 2), _BF16)]
        + [jax.ShapeDtypeStruct(w.shape, _BF16) for w in later_weights],
        scratch_shapes=[pltpu.VMEM((tm, D), _BF16)],
        compiler_params=_params("arbitrary", "arbitrary"),
        name="inproj",
    )(x2, gain, w_in, cos, sin, glu_bias, glu_bias, *later_weights)
    return outs[0], outs[1], outs[2:]


def _attn_kernel(lq1_ref, lk1_ref, lq2_ref, lk2_ref, g_ref, q_ref, k_ref, v_ref, o_ref,
                 s_ref, a_ref, part_ref, stat_ref, *, tq, lam_init):
    S = q_ref.shape[0]
    d = HEAD_DIM
    rc = BF16_ROWS
    nq = S // tq
    lam = (jnp.exp(jnp.sum(lq1_ref[...] * lk1_ref[...], axis=-1, keepdims=True))
           - jnp.exp(jnp.sum(lq2_ref[...] * lk2_ref[...], axis=-1, keepdims=True))
           + lam_init)
    row = lax.broadcasted_iota(jnp.int32, (tq, tq), 0)
    col = lax.broadcasted_iota(jnp.int32, (tq, tq), 1)
    causal = col <= row
    contract_last = (((1,), (1,)), ((), ()))

    def scores(i):
        off = i * tq
        end = off + tq
        sl = s_ref.at[i % 2]
        q = q_ref[off:end, :]
        for h in range(2):
            qh = q[:, h * d:(h + 1) * d]
            if off:
                sl[h, :, :off] = lax.dot_general(qh, k_ref[:off, h * d:(h + 1) * d], contract_last,
                                                 preferred_element_type=_F32)
            sd = lax.dot_general(qh, k_ref[off:end, h * d:(h + 1) * d], contract_last,
                                 preferred_element_type=_F32)
            sl[h, :, off:end] = jnp.where(causal, sd, NEG_INF)

    def output(i):
        off = i * tq
        end = off + tq
        half = tq // 2
        top = jnp.dot(a_ref[:half, :end - half], v_ref[:end - half, :], preferred_element_type=_F32)
        bot = jnp.dot(a_ref[half:, :end], v_ref[:end, :], preferred_element_type=_F32)
        for r0, o in ((off, top), (off + half, bot)):
            o = _rms(o, g_ref[...], SUBLN_EPS) * (1.0 - lam_init)
            o_ref[r0:r0 + half, :] = o.astype(_BF16)

    scores(0)
    for i in range(nq):
        off = i * tq
        end = off + tq
        ncol = end // LANES
        sl = s_ref.at[i % 2]

        def lane_block(ref, h, r0, c):
            return ref[h, pl.ds(r0, rc), c * LANES:(c + 1) * LANES]

        def segments(make_body):
            for g in range(tq // LANES):
                nvalid = off // LANES + g + 1
                body = make_body(nvalid)
                for r in range(g * (LANES // rc), (g + 1) * (LANES // rc)):
                    body(r, 0)

        def max_pass(nvalid):
            def body(r, _):
                r0 = r * rc
                for h in range(2):
                    m = [lane_block(sl, h, r0, c) for c in range(min(2, nvalid))]
                    for c in range(2, nvalid):
                        m[c % 2] = jnp.maximum(m[c % 2], lane_block(sl, h, r0, c))
                    part_ref[h, pl.ds(r0, rc), :] = m[0] if nvalid == 1 else jnp.maximum(m[0], m[1])
                return 0
            return body

        segments(max_pass)
        for h in range(2):
            m = jnp.max(part_ref[h], axis=-1, keepdims=True)
            stat_ref[h] = jnp.broadcast_to(m, (tq, LANES))
        if i + 1 < nq:
            scores(i + 1)

        def exp_pass(nvalid):
            def body(r, _):
                r0 = r * rc
                for h in range(2):
                    m = stat_ref[h, pl.ds(r0, rc), :]
                    acc = None
                    for c in range(nvalid):
                        p = jnp.exp2(lane_block(sl, h, r0, c) - m)
                        sl[h, pl.ds(r0, rc), c * LANES:(c + 1) * LANES] = p
                        acc = p if acc is None else acc + p
                    part_ref[h, pl.ds(r0, rc), :] = acc
                return 0
            return body

        segments(exp_pass)
        l1 = jnp.sum(part_ref[0], axis=-1, keepdims=True)
        l2 = jnp.sum(part_ref[1], axis=-1, keepdims=True)
        stat_ref[0] = jnp.broadcast_to(1.0 / l1, (tq, LANES))
        stat_ref[1] = jnp.broadcast_to(lam / l2, (tq, LANES))
        if i > 0:
            output(i - 1)

        def combine_pass(nvalid):
            def body(r, _):
                r0 = r * rc
                c1 = stat_ref[0, pl.ds(r0, rc), :]
                c2 = stat_ref[1, pl.ds(r0, rc), :]
                for c in range(nvalid):
                    a = lane_block(sl, 0, r0, c) * c1 - lane_block(sl, 1, r0, c) * c2
                    a_ref[pl.ds(r0, rc), c * LANES:(c + 1) * LANES] = a.astype(_BF16)
                if nvalid < ncol:
                    a_ref[pl.ds(r0, rc), nvalid * LANES:end] = jnp.zeros((rc, end - nvalid * LANES), _BF16)
                return 0
            return body

        segments(combine_pass)
    output(nq - 1)


def _attention(z, lq1, lk1, lq2, lk2, subln_gain, *, batch, seq, tq, lam_init):
    T = z.shape[0]
    hw = V_HEAD_DIM
    assert seq % tq == 0 and tq % LANES == 0
    kern = functools.partial(_attn_kernel, tq=tq, lam_init=lam_init)
    vec = pl.BlockSpec((1, HEAD_DIM), lambda b, h: (0, 0))
    return pl.pallas_call(
        kern,
        grid=(batch, N_HEADS),
        in_specs=[
            vec, vec, vec, vec,
            pl.BlockSpec((1, hw), lambda b, h: (0, 0)),
            pl.BlockSpec((seq, hw), lambda b, h: (b, h)),
            pl.BlockSpec((seq, hw), lambda b, h: (b, N_HEADS + h)),
            pl.BlockSpec((seq, hw), lambda b, h: (b, 2 * N_HEADS + h)),
        ],
        out_specs=pl.BlockSpec((seq, hw), lambda b, h: (b, h)),
        out_shape=jax.ShapeDtypeStruct((T, N_HEADS * hw), _BF16),
        scratch_shapes=[pltpu.VMEM((2, 2, tq, seq), _F32), pltpu.VMEM((tq, seq), _BF16),
                        pltpu.VMEM((2, tq, LANES), _F32), pltpu.VMEM((2, tq, LANES), _F32)],
        compiler_params=_params("parallel", "parallel"),
        name="diff_attn",
    )(lq1, lk1, lq2, lk2, subln_gain, z, z, z)


def _conv_merge_kernel(glu_ref, halo_ref, w_ref, b_ref, lg_ref, lb_ref,
                       a_ref, ga_ref, gc_ref, x_ref, wc_ref, bc_ref, wo_ref, gpost_ref, gff_ref,
                       h_ref, u_ref, g_scr, y_scr, c_scr, m_scr, yo_scr, *, rc, pw, sblocks, nblk):
    t = pl.program_id(0)
    tc, C = c_scr.shape
    D = yo_scr.shape[1]

    @pl.when(t == 0)
    def _():
        c_scr[...] = jnp.zeros_like(c_scr)

    nchunk = D // pw

    def merge_chunk(n):
        cols = slice(n * pw, (n + 1) * pw)
        conv_out = jnp.dot(c_scr[...], wc_ref[:, cols], preferred_element_type=_F32) + bc_ref[:, cols]
        m = (ga_ref[:, cols].astype(_F32) * a_ref[:, cols].astype(_F32)
             + gc_ref[:, cols].astype(_F32) * conv_out)
        m_scr[:, cols] = m.astype(_BF16)

    def out_chunk(n):
        cols = slice(n * pw, (n + 1) * pw)
        yo_scr[:, cols] = jnp.dot(m_scr[...], wo_ref[:, cols], preferred_element_type=_F32)

    pieces = ([(merge_chunk, n, c_scr) for n in range(nchunk)]
              + [(out_chunk, n, m_scr) for n in range(nchunk)])

    def pace(operand_ref, produced):
        bits = pltpu.bitcast(produced[:BF16_ROWS, :], jnp.uint32)
        zero = pltpu.bitcast(lax.shift_right_logical(lax.shift_right_logical(bits, jnp.uint32(31)),
                                                     jnp.uint32(1)), _F32)
        tile = operand_ref[:BF16_ROWS, :LANES]
        operand_ref[:BF16_ROWS, :LANES] = tile + zero.astype(tile.dtype)

    seq_start = jnp.minimum(t, nblk - 1) % sblocks == 0
    g_scr[:HALO, :] = jnp.where(seq_start, 0.0, halo_ref[...].astype(_F32))
    g_scr[HALO:, :] = glu_ref[...].astype(_F32)

    base = HALO - (CONV_WIDTH - 1)
    nchan = C // LANES
    stride = nchan // len(pieces)
    assert stride >= 1
    for ci in range(nchan):
        c0 = ci * LANES
        for r0 in range(0, tc, rc):
            win = g_scr[r0:r0 + rc + HALO, c0:c0 + LANES]
            acc = None
            for r in range(SUBLANES):
                part = None
                for k in range(CONV_WIDTH):
                    if (base + k) % SUBLANES == r:
                        term = w_ref[k:k + 1, c0:c0 + LANES] * win[base + k:base + k + rc, :]
                        part = term if part is None else part + term
                acc = part if acc is None else acc + part
            y_scr[r0:r0 + rc, c0:c0 + LANES] = acc
            if ci % stride == 0:
                for s0 in range(0, rc, BF16_ROWS):
                    done = acc[s0:s0 + BF16_ROWS, :] if r0 == 0 and s0 == 0 else done + acc[s0:s0 + BF16_ROWS, :]
        if ci % stride == 0 and ci // stride < len(pieces):
            fn, n, operand_ref = pieces[ci // stride]
            pace(operand_ref, done)
            fn(n)

    h = x_ref[...] + _rms(yo_scr[...], gpost_ref[...], NORM_EPS)
    h_ref[...] = h
    u_ref[...] = _rms(h, gff_ref[...], NORM_EPS).astype(_BF16)

    yb = y_scr[...] + b_ref[...]
    mu = jnp.mean(yb, axis=-1, keepdims=True)
    yc = yb - mu
    yn = yc * lax.rsqrt(jnp.mean(yc * yc, axis=-1, keepdims=True) + LN_EPS)
    yn = yn * lg_ref[...] + lb_ref[...]
    c_scr[...] = (yn * jax.nn.sigmoid(yn)).astype(_BF16)


def _conv_merge(glu, z, attn, x2, dw_kernel, dw_bias, ln_gain, ln_bias, w_conv_out, b_conv_out,
                w_out, g_post, g_ff, *, seq, tc, ga_col, rc=64, pw=512):
    T, D = x2.shape
    C = dw_bias.shape[1]
    assert seq % tc == 0 and T % tc == 0 and tc % HALO == 0 and tc % rc == 0
    assert glu.shape == (T, C) and ga_col % D == 0 and C == D and D % pw == 0
    gb = ga_col // D
    sblocks = seq // tc
    nblk = T // tc
    hpb = tc // HALO

    conv_blk = lambda t: jnp.minimum(t, nblk - 1)
    proj_blk = lambda t: jnp.maximum(t - 1, 0)
    conv_row = pl.BlockSpec((tc, C), lambda t: (conv_blk(t), 0))
    halo = pl.BlockSpec((HALO, C), lambda t: (jnp.maximum(conv_blk(t) * hpb - 1, 0), 0))
    proj_row = lambda col: pl.BlockSpec((tc, D), lambda t: (proj_blk(t), col))
    const = lambda shape: pl.BlockSpec(shape, lambda t: (0, 0), pipeline_mode=pl.Buffered(1))
    kern = functools.partial(_conv_merge_kernel, rc=rc, pw=pw, sblocks=sblocks, nblk=nblk)
    return pl.pallas_call(
        kern,
        grid=(nblk + 1,),
        in_specs=[conv_row, halo,
                  const((CONV_WIDTH, C)), const((1, C)), const((1, C)), const((1, C)),
                  proj_row(0), proj_row(gb), proj_row(gb + 1), proj_row(0),
                  const((C, D)), const((1, D)), const((D, D)), const((1, D)), const((1, D))],
        out_specs=[proj_row(0), proj_row(0)],
        out_shape=[jax.ShapeDtypeStruct((T, D), _F32), jax.ShapeDtypeStruct((T, D), _BF16)],
        scratch_shapes=[pltpu.VMEM((tc + HALO, C), _F32), pltpu.VMEM((tc, C), _F32),
                        pltpu.VMEM((tc, C), _BF16), pltpu.VMEM((tc, D), _BF16),
                        pltpu.VMEM((tc, D), _F32)],
        compiler_params=_params("arbitrary"),
        name="conv_merge",
    )(glu, glu, dw_kernel, dw_bias, ln_gain, ln_bias, attn, z, z, x2,
      w_conv_out, b_conv_out, w_out, g_post, g_ff)


def _ffn_kernel(u_ref, w1_ref, w2_ref, h_ref, g_ref, o_ref, acc_ref):
    j = pl.program_id(1)

    @pl.when(j == 0)
    def _():
        acc_ref[...] = jnp.zeros_like(acc_ref)

    a = jnp.dot(u_ref[...], w1_ref[...], preferred_element_type=_F32)
    a = jnp.square(jnp.maximum(a, 0.0)).astype(_BF16)
    acc_ref[...] += jnp.dot(a, w2_ref[...], preferred_element_type=_F32)

    @pl.when(j == pl.num_programs(1) - 1)
    def _():
        o_ref[...] = h_ref[...] + _rms(acc_ref[...], g_ref[...], NORM_EPS)


def _ffn(u2, w1, w2, h1, gain, *, tm, tf):
    T, D = h1.shape
    F = w1.shape[1]
    assert T % tm == 0 and F % tf == 0
    return pl.pallas_call(
        _ffn_kernel,
        grid=(T // tm, F // tf),
        in_specs=[
            pl.BlockSpec((tm, D), lambda i, j: (i, 0)),
            pl.BlockSpec((D, tf), lambda i, j: (0, j)),
            pl.BlockSpec((tf, D), lambda i, j: (j, 0)),
            pl.BlockSpec((tm, D), lambda i, j: (i, 0)),
            pl.BlockSpec((1, D), lambda i, j: (0, 0)),
        ],
        out_specs=pl.BlockSpec((tm, D), lambda i, j: (i, 0)),
        out_shape=jax.ShapeDtypeStruct((T, D), _F32),
        scratch_shapes=[pltpu.VMEM((tm, D), _F32)],
        compiler_params=_params("parallel", "arbitrary"),
        name="ffn",
    )(u2, w1, w2, h1, gain)


def _rope_tables(seq):
    d = HEAD_DIM
    inv_freq = 1.0 / (ROPE_THETA ** (jnp.arange(0, d, 2, dtype=_F32) / d))
    ang = jnp.arange(seq, dtype=_F32)[:, None] * inv_freq[None, :]
    cos = jnp.cos(ang)
    sin = jnp.sin(ang)
    return jnp.concatenate([cos, cos], axis=-1), jnp.concatenate([-sin, sin], axis=-1)


def _tiles(seq):
    return dict(inproj_rows=min(1024, seq), inproj_cols=1024, attn_queries=min(512, seq),
                conv_rows=min(256, seq), ffn_rows=min(512, seq), ffn_hidden=1024)


def _layer(h2d, l, p, *, batch, seq):
    D = h2d.shape[1]
    lam_init = 0.8 - 0.6 * math.exp(-0.3 * l)
    row = lambda a: a[l][None, :].astype(_F32)
    cos, sin = _rope_tables(seq)
    qk_w = N_HEADS * 2 * HEAD_DIM
    conv_col = 2 * qk_w + N_HEADS * V_HEAD_DIM
    conv_w = p['glu_bias'].shape[1]

    tiles = _tiles(seq)
    w_in = _paired_bf16(p['w_in'][l].astype(_F32), conv_col, conv_w)
    later = [p[name][l].astype(_F32) for name in ('w_conv_out', 'w_out', 'w_ff1', 'w_ff2')]
    z, glu, (w_conv_out, w_out, w_ff1, w_ff2) = _inproj(
        h2d, row(p['pre_mix_gain']), w_in, cos, sin, row(p['glu_bias']), later,
        seq=seq, tm=tiles['inproj_rows'], tn=tiles['inproj_cols'])
    attn = _attention(z, row(p['lambda_q1']), row(p['lambda_k1']), row(p['lambda_q2']),
                      row(p['lambda_k2']), row(p['subln_gain']),
                      batch=batch, seq=seq, tq=tiles['attn_queries'], lam_init=lam_init)
    h1, u2 = _conv_merge(glu, z, attn, h2d, p['dw_kernel'][l].astype(_F32), row(p['dw_bias']),
                         row(p['conv_ln_gain']), row(p['conv_ln_bias']),
                         w_conv_out, row(p['b_conv_out']),
                         w_out, row(p['post_mix_gain']),
                         row(p['pre_ff_gain']), seq=seq, tc=tiles['conv_rows'], ga_col=conv_col)
    return _ffn(u2, w_ff1, w_ff2, h1,
                row(p['post_ff_gain']), tm=tiles['ffn_rows'], tf=tiles['ffn_hidden'])


def kernel(x, pre_mix_gain, w_in, lambda_q1, lambda_k1, lambda_q2, lambda_k2, subln_gain, glu_bias,
           dw_kernel, dw_bias, conv_ln_gain, conv_ln_bias, w_conv_out, b_conv_out, w_out,
           post_mix_gain, pre_ff_gain, w_ff1, w_ff2, post_ff_gain):
    B, S, D = x.shape
    p = dict(pre_mix_gain=pre_mix_gain, w_in=w_in, lambda_q1=lambda_q1, lambda_k1=lambda_k1,
             lambda_q2=lambda_q2, lambda_k2=lambda_k2, subln_gain=subln_gain, glu_bias=glu_bias,
             dw_kernel=dw_kernel, dw_bias=dw_bias, conv_ln_gain=conv_ln_gain,
             conv_ln_bias=conv_ln_bias, w_conv_out=w_conv_out, b_conv_out=b_conv_out, w_out=w_out,
             post_mix_gain=post_mix_gain, pre_ff_gain=pre_ff_gain, w_ff1=w_ff1, w_ff2=w_ff2,
             post_ff_gain=post_ff_gain)
    h = x.reshape(B * S, D).astype(_F32)
    for l in range(w_in.shape[0]):
        h = _layer(h, l, p, batch=B, seq=S)
    return h.reshape(B, S, D).astype(x.dtype)
```

```python
import functools
import math

import jax
import jax.numpy as jnp
from jax import lax
from jax.experimental import pallas as pl
from jax.experimental.pallas import tpu as pltpu

N_HEADS = 8
HEAD_DIM = 128
V_HEAD_DIM = 2 * HEAD_DIM
ROPE_THETA = 10000.0
SUBLN_EPS = 1e-5
CONV_WIDTH = 31
LN_EPS = 1e-5
NORM_EPS = 1e-6
NEG_INF = -1e30
LOG2_E = 1.4426950408889634

LANES = 128
SUBLANES = 8
BF16_ROWS = 16
MXU_COLS = 256
GLU_PAIR = 2 * MXU_COLS
VMEM_LIMIT_BYTES = 56 * 1024 * 1024
HALO = 32

_BF16 = jnp.bfloat16
_F32 = jnp.float32


def _params(*semantics):
    return pltpu.CompilerParams(dimension_semantics=semantics, vmem_limit_bytes=VMEM_LIMIT_BYTES)


def _rms(v, gain, eps):
    return v * lax.rsqrt(jnp.mean(v * v, axis=-1, keepdims=True) + eps) * gain


def _inproj_kernel(x_ref, g_ref, w_ref, cos_ref, sin_ref, bv_ref, bg_ref, *rest,
                   tn, qk_tiles, v_tiles, conv_tiles, n_side):
    side_in = rest[:n_side]
    z_ref, glu_ref = rest[n_side:n_side + 2]
    side_out = rest[n_side + 2:2 * n_side + 2]
    u_ref = rest[2 * n_side + 2]
    j = pl.program_id(1)

    @pl.when(j == 0)
    def _():
        u_ref[...] = _rms(x_ref[...], g_ref[...], NORM_EPS).astype(_BF16)

    def matmul():
        for src, dst in zip(side_in, side_out):
            dst[...] = src[...].astype(_BF16)
        return jnp.dot(u_ref[...], w_ref[...], preferred_element_type=_F32)

    def rope_store(scale):
        acc = matmul()
        cos = cos_ref[...]
        sin = sin_ref[...]
        for c in range(tn // HEAD_DIM):
            sl = slice(c * HEAD_DIM, (c + 1) * HEAD_DIM)
            a = acc[:, sl]
            r = a * cos + pltpu.roll(a, HEAD_DIM // 2, axis=1) * sin
            if scale is not None:
                r = r * scale
            z_ref[:, sl] = r.astype(_BF16)

    def sigmoid(a):
        return 0.5 * jnp.tanh(0.5 * a) + 0.5

    q_end = qk_tiles
    k_end = 2 * qk_tiles
    v_end = k_end + v_tiles
    c_end = v_end + conv_tiles

    @pl.when(j < q_end)
    def _():
        rope_store(HEAD_DIM ** -0.5 * LOG2_E)

    @pl.when((j >= q_end) & (j < k_end))
    def _():
        rope_store(None)

    @pl.when((j >= k_end) & (j < v_end))
    def _():
        z_ref[...] = matmul().astype(_BF16)

    @pl.when((j >= v_end) & (j < c_end))
    def _():
        acc = matmul()
        half = GLU_PAIR // 2
        for c in range(tn // GLU_PAIR):
            out_cols = slice(c * half, (c + 1) * half)
            val = acc[:, c * GLU_PAIR:c * GLU_PAIR + half] + bv_ref[:, out_cols]
            gate = acc[:, c * GLU_PAIR + half:(c + 1) * GLU_PAIR] + bg_ref[:, out_cols]
            glu_ref[:, out_cols] = (val * sigmoid(gate)).astype(_BF16)

    @pl.when(j >= c_end)
    def _():
        z_ref[...] = sigmoid(matmul()).astype(_BF16)


def _cast_kernel(left_ref, right_ref, dst_ref):
    half = left_ref.shape[1]
    dst_ref[:, :half] = left_ref[...].astype(dst_ref.dtype)
    dst_ref[:, half:] = right_ref[...].astype(dst_ref.dtype)


def _paired_bf16(w_in, conv_col, conv_w):
    D, W = w_in.shape
    half = GLU_PAIR // 2
    assert W % GLU_PAIR == 0 and conv_col % GLU_PAIR == 0 and conv_w % GLU_PAIR == 0
    first, groups = conv_col // half, conv_w // GLU_PAIR

    def src_block(j):
        k = j - first
        paired = first + (k % 2) * groups + k // 2
        return jnp.where((k >= 0) & (k < 2 * groups), paired, j)

    return pl.pallas_call(
        _cast_kernel,
        grid=(W // GLU_PAIR,),
        in_specs=[pl.BlockSpec((D, half), lambda j: (0, src_block(2 * j))),
                  pl.BlockSpec((D, half), lambda j: (0, src_block(2 * j + 1)))],
        out_specs=pl.BlockSpec((D, GLU_PAIR), lambda j: (0, j)),
        out_shape=jax.ShapeDtypeStruct((D, W), _BF16),
        compiler_params=_params("parallel"),
        name="pair_cast",
    )(w_in, w_in)


def _inproj(x2, gain, w_in, cos, sin, glu_bias, later_weights, *, seq, tm, tn):
    T, D = x2.shape
    W = w_in.shape[1]
    qk_w = N_HEADS * 2 * HEAD_DIM
    v_w = N_HEADS * V_HEAD_DIM
    conv_w = glu_bias.shape[1]
    assert qk_w % tn == 0 and v_w % tn == 0 and conv_w % tn == 0 and W % tn == 0
    assert tn % GLU_PAIR == 0
    assert seq % tm == 0 and T % tm == 0
    qk_tiles, v_tiles, conv_tiles = qk_w // tn, v_w // tn, conv_w // tn
    conv0 = 2 * qk_tiles + v_tiles
    pos_blocks = seq // tm
    ni, nj = T // tm, W // tn
    z_tile = lambda j: jnp.where(j < conv0, j, jnp.maximum(j - conv_tiles, conv0))
    glu_tile = lambda j: jnp.clip(j - conv0, 0, conv_tiles - 1)
    nchunks = 1 << ((ni * nj).bit_length() - 1)
    for w in later_weights:
        nchunks = min(nchunks, w.shape[0] // BF16_ROWS)
    chunk_map = lambda i, j: (jnp.minimum(i * nj + j, nchunks - 1), 0)
    side_specs = []
    for w in later_weights:
        assert w.shape[0] % (nchunks * BF16_ROWS) == 0
        side_specs.append(pl.BlockSpec((w.shape[0] // nchunks, w.shape[1]), chunk_map))
    kern = functools.partial(_inproj_kernel, tn=tn, qk_tiles=qk_tiles, v_tiles=v_tiles,
                             conv_tiles=conv_tiles, n_side=len(later_weights))
    outs = pl.pallas_call(
        kern,
        grid=(ni, nj),
        in_specs=[
            pl.BlockSpec((tm, D), lambda i, j: (i, 0)),
            pl.BlockSpec((1, D), lambda i, j: (0, 0)),
            pl.BlockSpec((D, tn), lambda i, j: (0, j)),
            pl.BlockSpec((tm, HEAD_DIM), lambda i, j: (i % pos_blocks, 0)),
            pl.BlockSpec((tm, HEAD_DIM), lambda i, j: (i % pos_blocks, 0)),
            pl.BlockSpec((1, tn // 2), lambda i, j: (0, glu_tile(j))),
            pl.BlockSpec((1, tn // 2), lambda i, j: (0, conv_tiles + glu_tile(j))),
        ] + side_specs,
        out_specs=[pl.BlockSpec((tm, tn), lambda i, j: (i, z_tile(j))),
                   pl.BlockSpec((tm, tn // 2), lambda i, j: (i, glu_tile(j)))] + side_specs,
        out_shape=[jax.ShapeDtypeStruct((T, W - conv_w), _BF16),
                   jax.ShapeDtypeStruct((T, conv_w // 2), _BF16)]
        + [jax.ShapeDtypeStruct(w.shape, _BF16) for w in later_weights],
        scratch_shapes=[pltpu.VMEM((tm, D), _BF16)],
        compiler_params=_params("arbitrary", "arbitrary"),
        name="inproj",
    )(x2, gain, w_in, cos, sin, glu_bias, glu_bias, *later_weights)
    return outs[0], outs[1], outs[2:]


def _attn_kernel(lq1_ref, lk1_ref, lq2_ref, lk2_ref, g_ref, q_ref, k_ref, v_ref, o_ref,
                 *scratch, tq, lam_init):
    for hd in range(q_ref.shape[1] // V_HEAD_DIM):
        cols = pl.ds(hd * V_HEAD_DIM, V_HEAD_DIM)
        _attn_head(lq1_ref, lk1_ref, lq2_ref, lk2_ref, g_ref, q_ref.at[:, cols], k_ref.at[:, cols],
                   v_ref.at[:, cols], o_ref.at[:, cols], *scratch, tq=tq, lam_init=lam_init)


def _attn_head(lq1_ref, lk1_ref, lq2_ref, lk2_ref, g_ref, q_ref, k_ref, v_ref, o_ref,
               s_ref, a_ref, part_ref, stat_ref, *, tq, lam_init):
    S = q_ref.shape[0]
    d = HEAD_DIM
    rc = BF16_ROWS
    nq = S // tq
    lam = (jnp.exp(jnp.sum(lq1_ref[...] * lk1_ref[...], axis=-1, keepdims=True))
           - jnp.exp(jnp.sum(lq2_ref[...] * lk2_ref[...], axis=-1, keepdims=True))
           + lam_init)
    row = lax.broadcasted_iota(jnp.int32, (tq, tq), 0)
    col = lax.broadcasted_iota(jnp.int32, (tq, tq), 1)
    causal = col <= row
    contract_last = (((1,), (1,)), ((), ()))

    def scores(i):
        off = i * tq
        end = off + tq
        sl = s_ref.at[i % 2]
        q = q_ref[off:end, :]
        for h in range(2):
            qh = q[:, h * d:(h + 1) * d]
            if off:
                sl[h, :, :off] = lax.dot_general(qh, k_ref[:off, h * d:(h + 1) * d], contract_last,
                                                 preferred_element_type=_F32)
            sd = lax.dot_general(qh, k_ref[off:end, h * d:(h + 1) * d], contract_last,
                                 preferred_element_type=_F32)
            sl[h, :, off:end] = jnp.where(causal, sd, NEG_INF)

    def output(i):
        off = i * tq
        end = off + tq
        half = tq // 2
        top = jnp.dot(a_ref[:half, :end - half], v_ref[:end - half, :], preferred_element_type=_F32)
        bot = jnp.dot(a_ref[half:, :end], v_ref[:end, :], preferred_element_type=_F32)
        for r0, o in ((off, top), (off + half, bot)):
            o = _rms(o, g_ref[...], SUBLN_EPS) * (1.0 - lam_init)
            o_ref[r0:r0 + half, :] = o.astype(_BF16)

    scores(0)
    for i in range(nq):
        off = i * tq
        end = off + tq
        ncol = end // LANES
        sl = s_ref.at[i % 2]

        def lane_block(ref, h, r0, c):
            return ref[h, pl.ds(r0, rc), c * LANES:(c + 1) * LANES]

        def segments(make_body):
            for g in range(tq // LANES):
                nvalid = off // LANES + g + 1
                body = make_body(nvalid)
                for r in range(g * (LANES // rc), (g + 1) * (LANES // rc)):
                    body(r, 0)

        def max_pass(nvalid):
            def body(r, _):
                r0 = r * rc
                for h in range(2):
                    m = [lane_block(sl, h, r0, c) for c in range(min(2, nvalid))]
                    for c in range(2, nvalid):
                        m[c % 2] = jnp.maximum(m[c % 2], lane_block(sl, h, r0, c))
                    part_ref[h, pl.ds(r0, rc), :] = m[0] if nvalid == 1 else jnp.maximum(m[0], m[1])
                return 0
            return body

        segments(max_pass)
        for h in range(2):
            m = jnp.max(part_ref[h], axis=-1, keepdims=True)
            stat_ref[h] = jnp.broadcast_to(m, (tq, LANES))
        if i + 1 < nq:
            scores(i + 1)

        def exp_pass(nvalid):
            def body(r, _):
                r0 = r * rc
                for h in range(2):
                    m = stat_ref[h, pl.ds(r0, rc), :]
                    acc = None
                    for c in range(nvalid):
                        p = jnp.exp2(lane_block(sl, h, r0, c) - m)
                        sl[h, pl.ds(r0, rc), c * LANES:(c + 1) * LANES] = p
                        acc = p if acc is None else acc + p
                    part_ref[h, pl.ds(r0, rc), :] = acc
                return 0
            return body

        segments(exp_pass)
        l1 = jnp.sum(part_ref[0], axis=-1, keepdims=True)
        l2 = jnp.sum(part_ref[1], axis=-1, keepdims=True)
        stat_ref[0] = jnp.broadcast_to(1.0 / l1, (tq, LANES))
        stat_ref[1] = jnp.broadcast_to(lam / l2, (tq, LANES))
        if i > 0:
            output(i - 1)

        def combine_pass(nvalid):
            def body(r, _):
                r0 = r * rc
                c1 = stat_ref[0, pl.ds(r0, rc), :]
                c2 = stat_ref[1, pl.ds(r0, rc), :]
                for c in range(nvalid):
                    a = lane_block(sl, 0, r0, c) * c1 - lane_block(sl, 1, r0, c) * c2
                    a_ref[pl.ds(r0, rc), c * LANES:(c + 1) * LANES] = a.astype(_BF16)
                if nvalid < ncol:
                    a_ref[pl.ds(r0, rc), nvalid * LANES:end] = jnp.zeros((rc, end - nvalid * LANES), _BF16)
                return 0
            return body

        segments(combine_pass)
    output(nq - 1)


def _attention(z, lq1, lk1, lq2, lk2, subln_gain, *, batch, seq, tq, lam_init, heads=2):
    T = z.shape[0]
    hw = V_HEAD_DIM
    assert seq % tq == 0 and tq % LANES == 0 and N_HEADS % heads == 0
    groups = N_HEADS // heads
    kern = functools.partial(_attn_kernel, tq=tq, lam_init=lam_init)
    vec = pl.BlockSpec((1, HEAD_DIM), lambda b, h: (0, 0))
    return pl.pallas_call(
        kern,
        grid=(batch, groups),
        in_specs=[
            vec, vec, vec, vec,
            pl.BlockSpec((1, hw), lambda b, h: (0, 0)),
            pl.BlockSpec((seq, heads * hw), lambda b, h: (b, h)),
            pl.BlockSpec((seq, heads * hw), lambda b, h: (b, groups + h)),
            pl.BlockSpec((seq, heads * hw), lambda b, h: (b, 2 * groups + h)),
        ],
        out_specs=pl.BlockSpec((seq, heads * hw), lambda b, h: (b, h)),
        out_shape=jax.ShapeDtypeStruct((T, N_HEADS * hw), _BF16),
        scratch_shapes=[pltpu.VMEM((2, 2, tq, seq), _F32), pltpu.VMEM((tq, seq), _BF16),
                        pltpu.VMEM((2, tq, LANES), _F32), pltpu.VMEM((2, tq, LANES), _F32)],
        compiler_params=_params("parallel", "parallel"),
        name="diff_attn",
    )(lq1, lk1, lq2, lk2, subln_gain, z, z, z)


def _conv_merge_kernel(glu_ref, halo_ref, w_ref, b_ref, lg_ref, lb_ref,
                       a_ref, ga_ref, gc_ref, x_ref, wc_ref, bc_ref, wo_ref, gpost_ref, gff_ref,
                       h_ref, u_ref, g_scr, y_scr, c_scr, m_scr, yo_scr, *, rc, pw, sblocks, nblk):
    t = pl.program_id(0)
    tc, C = c_scr.shape
    D = yo_scr.shape[1]

    @pl.when(t == 0)
    def _():
        c_scr[...] = jnp.zeros_like(c_scr)

    nchunk = D // pw

    def merge_chunk(n):
        cols = slice(n * pw, (n + 1) * pw)
        conv_out = jnp.dot(c_scr[...], wc_ref[:, cols], preferred_element_type=_F32) + bc_ref[:, cols]
        m = (ga_ref[:, cols].astype(_F32) * a_ref[:, cols].astype(_F32)
             + gc_ref[:, cols].astype(_F32) * conv_out)
        m_scr[:, cols] = m.astype(_BF16)

    def out_chunk(n):
        cols = slice(n * pw, (n + 1) * pw)
        yo_scr[:, cols] = jnp.dot(m_scr[...], wo_ref[:, cols], preferred_element_type=_F32)

    pieces = ([(merge_chunk, n, c_scr) for n in range(nchunk)]
              + [(out_chunk, n, m_scr) for n in range(nchunk)])

    def pace(operand_ref, produced):
        bits = pltpu.bitcast(produced[:BF16_ROWS, :], jnp.uint32)
        zero = pltpu.bitcast(lax.shift_right_logical(lax.shift_right_logical(bits, jnp.uint32(31)),
                                                     jnp.uint32(1)), _F32)
        tile = operand_ref[:BF16_ROWS, :LANES]
        operand_ref[:BF16_ROWS, :LANES] = tile + zero.astype(tile.dtype)

    seq_start = jnp.minimum(t, nblk - 1) % sblocks == 0
    g_scr[:HALO, :] = jnp.where(seq_start, 0.0, halo_ref[...].astype(_F32))
    g_scr[HALO:, :] = glu_ref[...].astype(_F32)

    base = HALO - (CONV_WIDTH - 1)
    nchan = C // LANES
    stride = nchan // len(pieces)
    assert stride >= 1
    for ci in range(nchan):
        c0 = ci * LANES
        for r0 in range(0, tc, rc):
            win = g_scr[r0:r0 + rc + HALO, c0:c0 + LANES]
            acc = None
            for r in range(SUBLANES):
                part = None
                for k in range(CONV_WIDTH):
                    if (base + k) % SUBLANES == r:
                        term = w_ref[k:k + 1, c0:c0 + LANES] * win[base + k:base + k + rc, :]
                        part = term if part is None else part + term
                acc = part if acc is None else acc + part
            y_scr[r0:r0 + rc, c0:c0 + LANES] = acc
            if ci % stride == 0:
                for s0 in range(0, rc, BF16_ROWS):
                    done = acc[s0:s0 + BF16_ROWS, :] if r0 == 0 and s0 == 0 else done + acc[s0:s0 + BF16_ROWS, :]
        if ci % stride == 0 and ci // stride < len(pieces):
            fn, n, operand_ref = pieces[ci // stride]
            pace(operand_ref, done)
            fn(n)

    h = x_ref[...] + _rms(yo_scr[...], gpost_ref[...], NORM_EPS)
    h_ref[...] = h
    u_ref[...] = _rms(h, gff_ref[...], NORM_EPS).astype(_BF16)

    yb = y_scr[...] + b_ref[...]
    mu = jnp.mean(yb, axis=-1, keepdims=True)
    yc = yb - mu
    yn = yc * lax.rsqrt(jnp.mean(yc * yc, axis=-1, keepdims=True) + LN_EPS)
    yn = yn * lg_ref[...] + lb_ref[...]
    c_scr[...] = (yn * jax.nn.sigmoid(yn)).astype(_BF16)


def _conv_merge(glu, z, attn, x2, dw_kernel, dw_bias, ln_gain, ln_bias, w_conv_out, b_conv_out,
                w_out, g_post, g_ff, *, seq, tc, ga_col, rc=64, pw=512):
    T, D = x2.shape
    C = dw_bias.shape[1]
    assert seq % tc == 0 and T % tc == 0 and tc % HALO == 0 and tc % rc == 0
    assert glu.shape == (T, C) and ga_col % D == 0 and C == D and D % pw == 0
    gb = ga_col // D
    sblocks = seq // tc
    nblk = T // tc
    hpb = tc // HALO

    conv_blk = lambda t: jnp.minimum(t, nblk - 1)
    proj_blk = lambda t: jnp.maximum(t - 1, 0)
    conv_row = pl.BlockSpec((tc, C), lambda t: (conv_blk(t), 0))
    halo = pl.BlockSpec((HALO, C), lambda t: (jnp.maximum(conv_blk(t) * hpb - 1, 0), 0))
    proj_row = lambda col: pl.BlockSpec((tc, D), lambda t: (proj_blk(t), col))
    const = lambda shape: pl.BlockSpec(shape, lambda t: (0, 0), pipeline_mode=pl.Buffered(1))
    kern = functools.partial(_conv_merge_kernel, rc=rc, pw=pw, sblocks=sblocks, nblk=nblk)
    return pl.pallas_call(
        kern,
        grid=(nblk + 1,),
        in_specs=[conv_row, halo,
                  const((CONV_WIDTH, C)), const((1, C)), const((1, C)), const((1, C)),
                  proj_row(0), proj_row(gb), proj_row(gb + 1), proj_row(0),
                  const((C, D)), const((1, D)), const((D, D)), const((1, D)), const((1, D))],
        out_specs=[proj_row(0), proj_row(0)],
        out_shape=[jax.ShapeDtypeStruct((T, D), _F32), jax.ShapeDtypeStruct((T, D), _BF16)],
        scratch_shapes=[pltpu.VMEM((tc + HALO, C), _F32), pltpu.VMEM((tc, C), _F32),
                        pltpu.VMEM((tc, C), _BF16), pltpu.VMEM((tc, D), _BF16),
                        pltpu.VMEM((tc, D), _F32)],
        compiler_params=_params("arbitrary"),
        name="conv_merge",
    )(glu, glu, dw_kernel, dw_bias, ln_gain, ln_bias, attn, z, z, x2,
      w_conv_out, b_conv_out, w_out, g_post, g_ff)


def _ffn_kernel(u_ref, w1_ref, w2_ref, h_ref, g_ref, o_ref, acc_ref):
    j = pl.program_id(1)

    @pl.when(j == 0)
    def _():
        acc_ref[...] = jnp.zeros_like(acc_ref)

    a = jnp.dot(u_ref[...], w1_ref[...], preferred_element_type=_F32)
    a = jnp.square(jnp.maximum(a, 0.0)).astype(_BF16)
    acc_ref[...] += jnp.dot(a, w2_ref[...], preferred_element_type=_F32)

    @pl.when(j == pl.num_programs(1) - 1)
    def _():
        o_ref[...] = h_ref[...] + _rms(acc_ref[...], g_ref[...], NORM_EPS)


def _ffn(u2, w1, w2, h1, gain, *, tm, tf):
    T, D = h1.shape
    F = w1.shape[1]
    assert T % tm == 0 and F % tf == 0
    return pl.pallas_call(
        _ffn_kernel,
        grid=(T // tm, F // tf),
        in_specs=[
            pl.BlockSpec((tm, D), lambda i, j: (i, 0)),
            pl.BlockSpec((D, tf), lambda i, j: (0, j)),
            pl.BlockSpec((tf, D), lambda i, j: (j, 0)),
            pl.BlockSpec((tm, D), lambda i, j: (i, 0)),
            pl.BlockSpec((1, D), lambda i, j: (0, 0)),
        ],
        out_specs=pl.BlockSpec((tm, D), lambda i, j: (i, 0)),
        out_shape=jax.ShapeDtypeStruct((T, D), _F32),
        scratch_shapes=[pltpu.VMEM((tm, D), _F32)],
        compiler_params=_params("parallel", "arbitrary"),
        name="ffn",
    )(u2, w1, w2, h1, gain)


def _rope_tables(seq):
    d = HEAD_DIM
    inv_freq = 1.0 / (ROPE_THETA ** (jnp.arange(0, d, 2, dtype=_F32) / d))
    ang = jnp.arange(seq, dtype=_F32)[:, None] * inv_freq[None, :]
    cos = jnp.cos(ang)
    sin = jnp.sin(ang)
    return jnp.concatenate([cos, cos], axis=-1), jnp.concatenate([-sin, sin], axis=-1)


def _tiles(seq):
    return dict(inproj_rows=min(1024, seq), inproj_cols=1024, attn_queries=min(512, seq),
                conv_rows=min(256, seq), ffn_rows=min(512, seq), ffn_hidden=1024)


def _layer(h2d, l, p, *, batch, seq):
    D = h2d.shape[1]
    lam_init = 0.8 - 0.6 * math.exp(-0.3 * l)
    row = lambda a: a[l][None, :].astype(_F32)
    cos, sin = _rope_tables(seq)
    qk_w = N_HEADS * 2 * HEAD_DIM
    conv_col = 2 * qk_w + N_HEADS * V_HEAD_DIM
    conv_w = p['glu_bias'].shape[1]

    tiles = _tiles(seq)
    w_in = _paired_bf16(p['w_in'][l].astype(_F32), conv_col, conv_w)
    later = [p[name][l].astype(_F32) for name in ('w_conv_out', 'w_out', 'w_ff1', 'w_ff2')]
    z, glu, (w_conv_out, w_out, w_ff1, w_ff2) = _inproj(
        h2d, row(p['pre_mix_gain']), w_in, cos, sin, row(p['glu_bias']), later,
        seq=seq, tm=tiles['inproj_rows'], tn=tiles['inproj_cols'])
    attn = _attention(z, row(p['lambda_q1']), row(p['lambda_k1']), row(p['lambda_q2']),
                      row(p['lambda_k2']), row(p['subln_gain']),
                      batch=batch, seq=seq, tq=tiles['attn_queries'], lam_init=lam_init)
    h1, u2 = _conv_merge(glu, z, attn, h2d, p['dw_kernel'][l].astype(_F32), row(p['dw_bias']),
                         row(p['conv_ln_gain']), row(p['conv_ln_bias']),
                         w_conv_out, row(p['b_conv_out']),
                         w_out, row(p['post_mix_gain']),
                         row(p['pre_ff_gain']), seq=seq, tc=tiles['conv_rows'], ga_col=conv_col)
    return _ffn(u2, w_ff1, w_ff2, h1,
                row(p['post_ff_gain']), tm=tiles['ffn_rows'], tf=tiles['ffn_hidden'])


def kernel(x, pre_mix_gain, w_in, lambda_q1, lambda_k1, lambda_q2, lambda_k2, subln_gain, glu_bias,
           dw_kernel, dw_bias, conv_ln_gain, conv_ln_bias, w_conv_out, b_conv_out, w_out,
           post_mix_gain, pre_ff_gain, w_ff1, w_ff2, post_ff_gain):
    B, S, D = x.shape
    p = dict(pre_mix_gain=pre_mix_gain, w_in=w_in, lambda_q1=lambda_q1, lambda_k1=lambda_k1,
             lambda_q2=lambda_q2, lambda_k2=lambda_k2, subln_gain=subln_gain, glu_bias=glu_bias,
             dw_kernel=dw_kernel, dw_bias=dw_bias, conv_ln_gain=conv_ln_gain,
             conv_ln_bias=conv_ln_bias, w_conv_out=w_conv_out, b_conv_out=b_conv_out, w_out=w_out,
             post_mix_gain=post_mix_gain, pre_ff_gain=pre_ff_gain, w_ff1=w_ff1, w_ff2=w_ff2,
             post_ff_gain=post_ff_gain)
    h = x.reshape(B * S, D).astype(_F32)
    for l in range(w_in.shape[0]):
        h = _layer(h, l, p, batch=B, seq=S)
    return h.reshape(B, S, D).astype(x.dtype)
```

```python
import functools
import math

import jax
import jax.numpy as jnp
from jax import lax
from jax.experimental import pallas as pl
from jax.experimental.pallas import tpu as pltpu

N_HEADS = 8
HEAD_DIM = 128
V_HEAD_DIM = 2 * HEAD_DIM
ROPE_THETA = 10000.0
SUBLN_EPS = 1e-5
CONV_WIDTH = 31
LN_EPS = 1e-5
NORM_EPS = 1e-6
NEG_INF = -1e30
LOG2_E = 1.4426950408889634

LANES = 128
SUBLANES = 8
BF16_ROWS = 16
MXU_COLS = 256
GLU_PAIR = 2 * MXU_COLS
VMEM_LIMIT_BYTES = 56 * 1024 * 1024
HALO = 32

_BF16 = jnp.bfloat16
_F32 = jnp.float32


def _params(*semantics):
    return pltpu.CompilerParams(dimension_semantics=semantics, vmem_limit_bytes=VMEM_LIMIT_BYTES)


def _rms(v, gain, eps):
    return v * lax.rsqrt(jnp.mean(v * v, axis=-1, keepdims=True) + eps) * gain


def _inproj_kernel(x_ref, g_ref, w_ref, cos_ref, sin_ref, bv_ref, bg_ref, *rest,
                   tn, qk_tiles, v_tiles, conv_tiles, n_side):
    side_in = rest[:n_side]
    z_ref, glu_ref = rest[n_side:n_side + 2]
    side_out = rest[n_side + 2:2 * n_side + 2]
    u_ref = rest[2 * n_side + 2]
    j = pl.program_id(1)

    @pl.when(j == 0)
    def _():
        u_ref[...] = _rms(x_ref[...], g_ref[...], NORM_EPS).astype(_BF16)

    def matmul():
        for src, dst in zip(side_in, side_out):
            dst[...] = src[...].astype(_BF16)
        return jnp.dot(u_ref[...], w_ref[...], preferred_element_type=_F32)

    def rope_store(scale):
        acc = matmul()
        cos = cos_ref[...]
        sin = sin_ref[...]
        for c in range(tn // HEAD_DIM):
            sl = slice(c * HEAD_DIM, (c + 1) * HEAD_DIM)
            a = acc[:, sl]
            r = a * cos + pltpu.roll(a, HEAD_DIM // 2, axis=1) * sin
            if scale is not None:
                r = r * scale
            z_ref[:, sl] = r.astype(_BF16)

    def sigmoid(a):
        return 0.5 * jnp.tanh(0.5 * a) + 0.5

    q_end = qk_tiles
    k_end = 2 * qk_tiles
    v_end = k_end + v_tiles
    c_end = v_end + conv_tiles

    @pl.when(j < q_end)
    def _():
        rope_store(HEAD_DIM ** -0.5 * LOG2_E)

    @pl.when((j >= q_end) & (j < k_end))
    def _():
        rope_store(None)

    @pl.when((j >= k_end) & (j < v_end))
    def _():
        z_ref[...] = matmul().astype(_BF16)

    @pl.when((j >= v_end) & (j < c_end))
    def _():
        acc = matmul()
        half = GLU_PAIR // 2
        for c in range(tn // GLU_PAIR):
            out_cols = slice(c * half, (c + 1) * half)
            val = acc[:, c * GLU_PAIR:c * GLU_PAIR + half] + bv_ref[:, out_cols]
            gate = acc[:, c * GLU_PAIR + half:(c + 1) * GLU_PAIR] + bg_ref[:, out_cols]
            glu_ref[:, out_cols] = (val * sigmoid(gate)).astype(_BF16)

    @pl.when(j >= c_end)
    def _():
        z_ref[...] = sigmoid(matmul()).astype(_BF16)


def _cast_kernel(left_ref, right_ref, dst_ref):
    half = left_ref.shape[1]
    dst_ref[:, :half] = left_ref[...].astype(dst_ref.dtype)
    dst_ref[:, half:] = right_ref[...].astype(dst_ref.dtype)


def _paired_bf16(w_in, conv_col, conv_w):
    D, W = w_in.shape
    half = GLU_PAIR // 2
    assert W % GLU_PAIR == 0 and conv_col % GLU_PAIR == 0 and conv_w % GLU_PAIR == 0
    first, groups = conv_col // half, conv_w // GLU_PAIR

    def src_block(j):
        k = j - first
        paired = first + (k % 2) * groups + k // 2
        return jnp.where((k >= 0) & (k < 2 * groups), paired, j)

    return pl.pallas_call(
        _cast_kernel,
        grid=(W // GLU_PAIR,),
        in_specs=[pl.BlockSpec((D, half), lambda j: (0, src_block(2 * j))),
                  pl.BlockSpec((D, half), lambda j: (0, src_block(2 * j + 1)))],
        out_specs=pl.BlockSpec((D, GLU_PAIR), lambda j: (0, j)),
        out_shape=jax.ShapeDtypeStruct((D, W), _BF16),
        compiler_params=_params("parallel"),
        name="pair_cast",
    )(w_in, w_in)


def _inproj(x2, gain, w_in, cos, sin, glu_bias, later_weights, *, seq, tm, tn):
    T, D = x2.shape
    W = w_in.shape[1]
    qk_w = N_HEADS * 2 * HEAD_DIM
    v_w = N_HEADS * V_HEAD_DIM
    conv_w = glu_bias.shape[1]
    assert qk_w % tn == 0 and v_w % tn == 0 and conv_w % tn == 0 and W % tn == 0
    assert tn % GLU_PAIR == 0
    assert seq % tm == 0 and T % tm == 0
    qk_tiles, v_tiles, conv_tiles = qk_w // tn, v_w // tn, conv_w // tn
    conv0 = 2 * qk_tiles + v_tiles
    pos_blocks = seq // tm
    ni, nj = T // tm, W // tn
    z_tile = lambda j: jnp.where(j < conv0, j, jnp.maximum(j - conv_tiles, conv0))
    glu_tile = lambda j: jnp.clip(j - conv0, 0, conv_tiles - 1)
    nchunks = 1 << ((ni * nj).bit_length() - 1)
    for w in later_weights:
        nchunks = min(nchunks, w.shape[0] // BF16_ROWS)
    chunk_map = lambda i, j: (jnp.minimum(i * nj + j, nchunks - 1), 0)
    side_specs = []
    for w in later_weights:
        assert w.shape[0] % (nchunks * BF16_ROWS) == 0
        side_specs.append(pl.BlockSpec((w.shape[0] // nchunks, w.shape[1]), chunk_map))
    kern = functools.partial(_inproj_kernel, tn=tn, qk_tiles=qk_tiles, v_tiles=v_tiles,
                             conv_tiles=conv_tiles, n_side=len(later_weights))
    outs = pl.pallas_call(
        kern,
        grid=(ni, nj),
        in_specs=[
            pl.BlockSpec((tm, D), lambda i, j: (i, 0)),
            pl.BlockSpec((1, D), lambda i, j: (0, 0)),
            pl.BlockSpec((D, tn), lambda i, j: (0, j)),
            pl.BlockSpec((tm, HEAD_DIM), lambda i, j: (i % pos_blocks, 0)),
            pl.BlockSpec((tm, HEAD_DIM), lambda i, j: (i % pos_blocks, 0)),
            pl.BlockSpec((1, tn // 2), lambda i, j: (0, glu_tile(j))),
            pl.BlockSpec((1, tn // 2), lambda i, j: (0, conv_tiles + glu_tile(j))),
        ] + side_specs,
        out_specs=[pl.BlockSpec((tm, tn), lambda i, j: (i, z_tile(j))),
                   pl.BlockSpec((tm, tn // 2), lambda i, j: (i, glu_tile(j)))] + side_specs,
        out_shape=[jax.ShapeDtypeStruct((T, W - conv_w), _BF16),
                   jax.ShapeDtypeStruct((T, conv_w // 2), _BF16)]
        + [jax.ShapeDtypeStruct(w.shape, _BF16) for w in later_weights],
        scratch_shapes=[pltpu.VMEM((tm, D), _BF16)],
        compiler_params=_params("arbitrary", "arbitrary"),
        name="inproj",
    )(x2, gain, w_in, cos, sin, glu_bias, glu_bias, *later_weights)
    return outs[0], outs[1], outs[2:]


def _attn_kernel(lq1_ref, lk1_ref, lq2_ref, lk2_ref, g_ref, q_ref, k_ref, v_ref, o_ref,
                 s_ref, a_ref, part_ref, stat_ref, *, tq, lam_init):
    S = q_ref.shape[0]
    d = HEAD_DIM
    rc = BF16_ROWS
    nq = S // tq
    lam = (jnp.exp(jnp.sum(lq1_ref[...] * lk1_ref[...], axis=-1, keepdims=True))
           - jnp.exp(jnp.sum(lq2_ref[...] * lk2_ref[...], axis=-1, keepdims=True))
           + lam_init)
    row = lax.broadcasted_iota(jnp.int32, (tq, tq), 0)
    col = lax.broadcasted_iota(jnp.int32, (tq, tq), 1)
    causal = col <= row
    contract_last = (((1,), (1,)), ((), ()))

    def scores(i):
        off = i * tq
        end = off + tq
        sl = s_ref.at[i % 2]
        q = q_ref[off:end, :]
        for h in range(2):
            qh = q[:, h * d:(h + 1) * d]
            if off:
                sl[h, :, :off] = lax.dot_general(qh, k_ref[:off, h * d:(h + 1) * d], contract_last,
                                                 preferred_element_type=_F32)
            sd = lax.dot_general(qh, k_ref[off:end, h * d:(h + 1) * d], contract_last,
                                 preferred_element_type=_F32)
            sl[h, :, off:end] = jnp.where(causal, sd, NEG_INF)

    def output(i):
        off = i * tq
        end = off + tq
        half = tq // 2
        top = jnp.dot(a_ref[:half, :end - half], v_ref[:end - half, :], preferred_element_type=_F32)
        bot = jnp.dot(a_ref[half:, :end], v_ref[:end, :], preferred_element_type=_F32)
        for r0, o in ((off, top), (off + half, bot)):
            o = _rms(o, g_ref[...], SUBLN_EPS) * (1.0 - lam_init)
            o_ref[r0:r0 + half, :] = o.astype(_BF16)

    scores(0)
    for i in range(nq):
        off = i * tq
        end = off + tq
        ncol = end // LANES
        sl = s_ref.at[i % 2]

        def lane_block(ref, h, r0, c):
            return ref[h, pl.ds(r0, rc), c * LANES:(c + 1) * LANES]

        def segments(make_body):
            for g in range(tq // LANES):
                nvalid = off // LANES + g + 1
                body = make_body(nvalid)
                for r in range(g * (LANES // rc), (g + 1) * (LANES // rc)):
                    body(r, 0)

        def max_pass(nvalid):
            def body(r, _):
                r0 = r * rc
                for h in range(2):
                    m = [lane_block(sl, h, r0, c) for c in range(min(2, nvalid))]
                    for c in range(2, nvalid):
                        m[c % 2] = jnp.maximum(m[c % 2], lane_block(sl, h, r0, c))
                    part_ref[h, pl.ds(r0, rc), :] = m[0] if nvalid == 1 else jnp.maximum(m[0], m[1])
                return 0
            return body

        segments(max_pass)
        for h in range(2):
            m = jnp.max(part_ref[h], axis=-1, keepdims=True)
            stat_ref[h] = jnp.broadcast_to(m, (tq, LANES))
        if i + 1 < nq:
            scores(i + 1)

        def exp_pass(nvalid):
            def body(r, _):
                r0 = r * rc
                for h in range(2):
                    m = stat_ref[h, pl.ds(r0, rc), :]
                    acc = None
                    for c in range(nvalid):
                        p = jnp.exp2(lane_block(sl, h, r0, c) - m)
                        sl[h, pl.ds(r0, rc), c * LANES:(c + 1) * LANES] = p
                        acc = p if acc is None else acc + p
                    part_ref[h, pl.ds(r0, rc), :] = acc
                return 0
            return body

        segments(exp_pass)
        l1 = jnp.sum(part_ref[0], axis=-1, keepdims=True)
        l2 = jnp.sum(part_ref[1], axis=-1, keepdims=True)
        stat_ref[0] = jnp.broadcast_to(1.0 / l1, (tq, LANES))
        stat_ref[1] = jnp.broadcast_to(lam / l2, (tq, LANES))
        if i > 0:
            output(i - 1)

        def combine_pass(nvalid):
            def body(r, _):
                r0 = r * rc
                c1 = stat_ref[0, pl.ds(r0, rc), :]
                c2 = stat_ref[1, pl.ds(r0, rc), :]
                for c in range(nvalid):
                    a = lane_block(sl, 0, r0, c) * c1 - lane_block(sl, 1, r0, c) * c2
                    a_ref[pl.ds(r0, rc), c * LANES:(c + 1) * LANES] = a.astype(_BF16)
                if nvalid < ncol:
                    a_ref[pl.ds(r0, rc), nvalid * LANES:end] = jnp.zeros((rc, end - nvalid * LANES), _BF16)
                return 0
            return body

        segments(combine_pass)
    output(nq - 1)


def _attention(z, lq1, lk1, lq2, lk2, subln_gain, *, batch, seq, tq, lam_init):
    T = z.shape[0]
    hw = V_HEAD_DIM
    assert seq % tq == 0 and tq % LANES == 0
    kern = functools.partial(_attn_kernel, tq=tq, lam_init=lam_init)
    vec = pl.BlockSpec((1, HEAD_DIM), lambda b, h: (0, 0))
    return pl.pallas_call(
        kern,
        grid=(batch, N_HEADS),
        in_specs=[
            vec, vec, vec, vec,
            pl.BlockSpec((1, hw), lambda b, h: (0, 0)),
            pl.BlockSpec((seq, hw), lambda b, h: (b, h)),
            pl.BlockSpec((seq, hw), lambda b, h: (b, N_HEADS + h)),
            pl.BlockSpec((seq, hw), lambda b, h: (b, 2 * N_HEADS + h)),
        ],
        out_specs=pl.BlockSpec((seq, hw), lambda b, h: (b, h)),
        out_shape=jax.ShapeDtypeStruct((T, N_HEADS * hw), _BF16),
        scratch_shapes=[pltpu.VMEM((2, 2, tq, seq), _F32), pltpu.VMEM((tq, seq), _BF16),
                        pltpu.VMEM((2, tq, LANES), _F32), pltpu.VMEM((2, tq, LANES), _F32)],
        compiler_params=_params("parallel", "parallel"),
        name="diff_attn",
    )(lq1, lk1, lq2, lk2, subln_gain, z, z, z)


def _conv_merge_kernel(glu_ref, halo_ref, w_ref, b_ref, lg_ref, lb_ref,
                       a_ref, ga_ref, gc_ref, x_ref, wc_ref, bc_ref, wo_ref, gpost_ref, gff_ref,
                       h_ref, u_ref, g_scr, y_scr, c_scr, m_scr, yo_scr, *, rc, pw, sblocks, nblk):
    t = pl.program_id(0)
    tc, C = c_scr.shape
    D = yo_scr.shape[1]

    @pl.when(t == 0)
    def _():
        c_scr[...] = jnp.zeros_like(c_scr)

    nchunk = D // pw

    def merge_chunk(n):
        cols = slice(n * pw, (n + 1) * pw)
        conv_out = jnp.dot(c_scr[...], wc_ref[:, cols], preferred_element_type=_F32) + bc_ref[:, cols]
        m = (ga_ref[:, cols].astype(_F32) * a_ref[:, cols].astype(_F32)
             + gc_ref[:, cols].astype(_F32) * conv_out)
        m_scr[:, cols] = m.astype(_BF16)

    def out_chunk(n):
        cols = slice(n * pw, (n + 1) * pw)
        yo_scr[:, cols] = jnp.dot(m_scr[...], wo_ref[:, cols], preferred_element_type=_F32)

    pieces = ([(merge_chunk, n, c_scr) for n in range(nchunk)]
              + [(out_chunk, n, m_scr) for n in range(nchunk)])

    def pace(operand_ref, produced):
        bits = pltpu.bitcast(produced[:BF16_ROWS, :], jnp.uint32)
        zero = pltpu.bitcast(lax.shift_right_logical(lax.shift_right_logical(bits, jnp.uint32(31)),
                                                     jnp.uint32(1)), _F32)
        tile = operand_ref[:BF16_ROWS, :LANES]
        operand_ref[:BF16_ROWS, :LANES] = tile + zero.astype(tile.dtype)

    seq_start = jnp.minimum(t, nblk - 1) % sblocks == 0
    g_scr[:HALO, :] = jnp.where(seq_start, 0.0, halo_ref[...].astype(_F32))
    g_scr[HALO:, :] = glu_ref[...].astype(_F32)

    base = HALO - (CONV_WIDTH - 1)
    nchan = C // LANES
    stride = nchan // len(pieces)
    assert stride >= 1
    for ci in range(nchan):
        c0 = ci * LANES
        for r0 in range(0, tc, rc):
            win = g_scr[r0:r0 + rc + HALO, c0:c0 + LANES]
            acc = None
            for r in range(SUBLANES):
                part = None
                for k in range(CONV_WIDTH):
                    if (base + k) % SUBLANES == r:
                        term = w_ref[k:k + 1, c0:c0 + LANES] * win[base + k:base + k + rc, :]
                        part = term if part is None else part + term
                acc = part if acc is None else acc + part
            y_scr[r0:r0 + rc, c0:c0 + LANES] = acc
            if ci % stride == 0:
                for s0 in range(0, rc, BF16_ROWS):
                    done = acc[s0:s0 + BF16_ROWS, :] if r0 == 0 and s0 == 0 else done + acc[s0:s0 + BF16_ROWS, :]
        if ci % stride == 0 and ci // stride < len(pieces):
            fn, n, operand_ref = pieces[ci // stride]
            pace(operand_ref, done)
            fn(n)

    h = x_ref[...] + _rms(yo_scr[...], gpost_ref[...], NORM_EPS)
    h_ref[...] = h
    u_ref[...] = _rms(h, gff_ref[...], NORM_EPS).astype(_BF16)

    yb = y_scr[...] + b_ref[...]
    mu = jnp.mean(yb, axis=-1, keepdims=True)
    yc = yb - mu
    yn = yc * lax.rsqrt(jnp.mean(yc * yc, axis=-1, keepdims=True) + LN_EPS)
    yn = yn * lg_ref[...] + lb_ref[...]
    c_scr[...] = (yn * jax.nn.sigmoid(yn)).astype(_BF16)


def _conv_merge(glu, z, attn, x2, dw_kernel, dw_bias, ln_gain, ln_bias, w_conv_out, b_conv_out,
                w_out, g_post, g_ff, *, seq, tc, ga_col, rc=64, pw=512):
    T, D = x2.shape
    C = dw_bias.shape[1]
    assert seq % tc == 0 and T % tc == 0 and tc % HALO == 0 and tc % rc == 0
    assert glu.shape == (T, C) and ga_col % D == 0 and C == D and D % pw == 0
    gb = ga_col // D
    sblocks = seq // tc
    nblk = T // tc
    hpb = tc // HALO

    conv_blk = lambda t: jnp.minimum(t, nblk - 1)
    proj_blk = lambda t: jnp.maximum(t - 1, 0)
    conv_row = pl.BlockSpec((tc, C), lambda t: (conv_blk(t), 0))
    halo = pl.BlockSpec((HALO, C), lambda t: (jnp.maximum(conv_blk(t) * hpb - 1, 0), 0))
    proj_row = lambda col: pl.BlockSpec((tc, D), lambda t: (proj_blk(t), col))
    const = lambda shape: pl.BlockSpec(shape, lambda t: (0, 0), pipeline_mode=pl.Buffered(1))
    kern = functools.partial(_conv_merge_kernel, rc=rc, pw=pw, sblocks=sblocks, nblk=nblk)
    return pl.pallas_call(
        kern,
        grid=(nblk + 1,),
        in_specs=[conv_row, halo,
                  const((CONV_WIDTH, C)), const((1, C)), const((1, C)), const((1, C)),
                  proj_row(0), proj_row(gb), proj_row(gb + 1), proj_row(0),
                  const((C, D)), const((1, D)), const((D, D)), const((1, D)), const((1, D))],
        out_specs=[proj_row(0), proj_row(0)],
        out_shape=[jax.ShapeDtypeStruct((T, D), _F32), jax.ShapeDtypeStruct((T, D), _BF16)],
        scratch_shapes=[pltpu.VMEM((tc + HALO, C), _F32), pltpu.VMEM((tc, C), _F32),
                        pltpu.VMEM((tc, C), _BF16), pltpu.VMEM((tc, D), _BF16),
                        pltpu.VMEM((tc, D), _F32)],
        compiler_params=_params("arbitrary"),
        name="conv_merge",
    )(glu, glu, dw_kernel, dw_bias, ln_gain, ln_bias, attn, z, z, x2,
      w_conv_out, b_conv_out, w_out, g_post, g_ff)


def _ffn_kernel(u_ref, w1_ref, w2_ref, h_ref, g_ref, o_ref, acc_ref):
    j = pl.program_id(1)
    last = pl.num_programs(1) - 1

    def partial_product():
        a = jnp.dot(u_ref[...], w1_ref[...], preferred_element_type=_F32)
        a = jnp.square(jnp.maximum(a, 0.0)).astype(_BF16)
        return jnp.dot(a, w2_ref[...], preferred_element_type=_F32)

    @pl.when(j == 0)
    def _():
        acc_ref[...] = partial_product()

    @pl.when((j > 0) & (j < last))
    def _():
        acc_ref[...] += partial_product()

    @pl.when(j == last)
    def _():
        total = acc_ref[...] + partial_product()
        o_ref[...] = h_ref[...] + _rms(total, g_ref[...], NORM_EPS)


def _ffn(u2, w1, w2, h1, gain, *, tm, tf):
    T, D = h1.shape
    F = w1.shape[1]
    assert T % tm == 0 and F % tf == 0 and F // tf >= 2
    return pl.pallas_call(
        _ffn_kernel,
        grid=(T // tm, F // tf),
        in_specs=[
            pl.BlockSpec((tm, D), lambda i, j: (i, 0)),
            pl.BlockSpec((D, tf), lambda i, j: (0, j)),
            pl.BlockSpec((tf, D), lambda i, j: (j, 0)),
            pl.BlockSpec((tm, D), lambda i, j: (i, 0)),
            pl.BlockSpec((1, D), lambda i, j: (0, 0)),
        ],
        out_specs=pl.BlockSpec((tm, D), lambda i, j: (i, 0)),
        out_shape=jax.ShapeDtypeStruct((T, D), _F32),
        scratch_shapes=[pltpu.VMEM((tm, D), _F32)],
        compiler_params=_params("parallel", "arbitrary"),
        name="ffn",
    )(u2, w1, w2, h1, gain)


def _rope_tables(seq):
    d = HEAD_DIM
    inv_freq = 1.0 / (ROPE_THETA ** (jnp.arange(0, d, 2, dtype=_F32) / d))
    ang = jnp.arange(seq, dtype=_F32)[:, None] * inv_freq[None, :]
    cos = jnp.cos(ang)
    sin = jnp.sin(ang)
    return jnp.concatenate([cos, cos], axis=-1), jnp.concatenate([-sin, sin], axis=-1)


def _tiles(seq):
    return dict(inproj_rows=min(1024, seq), inproj_cols=1024, attn_queries=min(512, seq),
                conv_rows=min(256, seq), ffn_rows=min(512, seq), ffn_hidden=1024)


def _layer(h2d, l, p, *, batch, seq):
    D = h2d.shape[1]
    lam_init = 0.8 - 0.6 * math.exp(-0.3 * l)
    row = lambda a: a[l][None, :].astype(_F32)
    cos, sin = _rope_tables(seq)
    qk_w = N_HEADS * 2 * HEAD_DIM
    conv_col = 2 * qk_w + N_HEADS * V_HEAD_DIM
    conv_w = p['glu_bias'].shape[1]

    tiles = _tiles(seq)
    w_in = _paired_bf16(p['w_in'][l].astype(_F32), conv_col, conv_w)
    later = [p[name][l].astype(_F32) for name in ('w_conv_out', 'w_out', 'w_ff1', 'w_ff2')]
    z, glu, (w_conv_out, w_out, w_ff1, w_ff2) = _inproj(
        h2d, row(p['pre_mix_gain']), w_in, cos, sin, row(p['glu_bias']), later,
        seq=seq, tm=tiles['inproj_rows'], tn=tiles['inproj_cols'])
    attn = _attention(z, row(p['lambda_q1']), row(p['lambda_k1']), row(p['lambda_q2']),
                      row(p['lambda_k2']), row(p['subln_gain']),
                      batch=batch, seq=seq, tq=tiles['attn_queries'], lam_init=lam_init)
    h1, u2 = _conv_merge(glu, z, attn, h2d, p['dw_kernel'][l].astype(_F32), row(p['dw_bias']),
                         row(p['conv_ln_gain']), row(p['conv_ln_bias']),
                         w_conv_out, row(p['b_conv_out']),
                         w_out, row(p['post_mix_gain']),
                         row(p['pre_ff_gain']), seq=seq, tc=tiles['conv_rows'], ga_col=conv_col)
    return _ffn(u2, w_ff1, w_ff2, h1,
                row(p['post_ff_gain']), tm=tiles['ffn_rows'], tf=tiles['ffn_hidden'])


def kernel(x, pre_mix_gain, w_in, lambda_q1, lambda_k1, lambda_q2, lambda_k2, subln_gain, glu_bias,
           dw_kernel, dw_bias, conv_ln_gain, conv_ln_bias, w_conv_out, b_conv_out, w_out,
           post_mix_gain, pre_ff_gain, w_ff1, w_ff2, post_ff_gain):
    B, S, D = x.shape
    p = dict(pre_mix_gain=pre_mix_gain, w_in=w_in, lambda_q1=lambda_q1, lambda_k1=lambda_k1,
             lambda_q2=lambda_q2, lambda_k2=lambda_k2, subln_gain=subln_gain, glu_bias=glu_bias,
             dw_kernel=dw_kernel, dw_bias=dw_bias, conv_ln_gain=conv_ln_gain,
             conv_ln_bias=conv_ln_bias, w_conv_out=w_conv_out, b_conv_out=b_conv_out, w_out=w_out,
             post_mix_gain=post_mix_gain, pre_ff_gain=pre_ff_gain, w_ff1=w_ff1, w_ff2=w_ff2,
             post_ff_gain=post_ff_gain)
    h = x.reshape(B * S, D).astype(_F32)
    for l in range(w_in.shape[0]):
        h = _layer(h, l, p, batch=B, seq=S)
    return h.reshape(B, S, D).astype(x.dtype)
```
